```python
import jax, jax.numpy as jnp
from jax import lax
import numpy as np

D_MODEL = 1024
BATCH = 4
SEQ = 8192
DEPTH = 4
DEC_BATCH = 32
DEC_SEQ = 16
PAST_LEN = 1024

CHUNK = 64
Q_BLOCK = 128
N_MIXERS = 2
N_MLA = (DEPTH + 1) // 2
N_SB = DEPTH // 2
MLA_HEADS = 8
MLA_NOPE = 128
MLA_ROPE = 64
MLA_V = 128
MLA_Q_LORA = 512
MLA_KV_LORA = 256
MLA_IN = MLA_Q_LORA + MLA_KV_LORA + MLA_ROPE
ROPE_THETA = 10000.0
SB_HEADS = 16
SB_HEAD_DIM = 64
SB_WIDTH = SB_HEADS * SB_HEAD_DIM
D_FF = 4 * D_MODEL
EPS = 1e-6

kernel_name = 'hybrid_mla_stickbreaking_streaming_step'


def rmsnorm(x, g):
    xf = x.astype(jnp.float32)
    y = xf * lax.rsqrt(jnp.mean(xf * xf, axis=-1, keepdims=True) + EPS)
    return (y * g.astype(jnp.float32)).astype(x.dtype)


def rope_tables(pos):
    inv = ROPE_THETA ** (-jnp.arange(0, MLA_ROPE, 2, dtype=jnp.float32) / MLA_ROPE)
    ang = pos.astype(jnp.float32)[:, None] * inv[None, :]
    return jnp.cos(ang), jnp.sin(ang)


def apply_rope(x, cos, sin):
    half = x.shape[-1] // 2
    x1 = x[..., :half].astype(jnp.float32)
    x2 = x[..., half:].astype(jnp.float32)
    return jnp.concatenate([x1 * cos - x2 * sin, x2 * cos + x1 * sin], axis=-1).astype(x.dtype)


def mla_attend(q_nope, q_rope, q_pos, k_nope, k_rope, v, k_pos):
    scale = (MLA_NOPE + MLA_ROPE) ** -0.5
    s = (jnp.einsum('bqhd,bkhd->bhqk', q_nope, k_nope, preferred_element_type=jnp.float32)
         + jnp.einsum('bqhr,bkr->bhqk', q_rope, k_rope, preferred_element_type=jnp.float32)) * scale
    mask = (k_pos[None, :] // CHUNK) <= (q_pos[:, None] // CHUNK)
    s = jnp.where(mask[None, None], s, -jnp.inf)
    p = jax.nn.softmax(s, axis=-1)
    return jnp.einsum('bhqk,bkhd->bqhd', p.astype(v.dtype), v)


def sb_attend(q, q_pos, k, v, k_pos):
    z = jnp.einsum('bqhd,bkhd->bhqk', q, k, preferred_element_type=jnp.float32) * (SB_HEAD_DIM ** -0.5)
    mask = (k_pos[None, :] < q_pos[:, None])[None, None]
    log_keep = jnp.where(mask, jax.nn.log_sigmoid(-z), 0.0)
    suffix = lax.cumsum(log_keep, axis=3, reverse=True) - log_keep
    a = jnp.where(mask, jnp.exp(jax.nn.log_sigmoid(z) + suffix), 0.0)
    return jnp.einsum('bhqk,bkhd->bqhd', a.astype(v.dtype), v)


def attend_queries(attend, q_parts, q_pos, kv_args):
    B, S = q_parts[0].shape[:2]
    if S >= Q_BLOCK and S % Q_BLOCK == 0:
        nb = S // Q_BLOCK
        qb = tuple(q.reshape((B, nb, Q_BLOCK) + q.shape[2:]).swapaxes(0, 1) for q in q_parts)
        pb = q_pos.reshape(nb, Q_BLOCK)
        out = lax.map(lambda xs: attend(*xs[0], xs[1], *kv_args), (qb, pb))
        return out.swapaxes(0, 1).reshape((B, S) + out.shape[3:])
    return attend(*q_parts, q_pos, *kv_args)


def mla_mixer(h, q_pos, past_ckv, past_krope, w_in, g_q, w_qb, g_kv, w_kvb, w_o):
    B, S, _ = h.shape
    a = h @ w_in
    c_q = rmsnorm(a[..., :MLA_Q_LORA], g_q)
    c_kv = rmsnorm(a[..., MLA_Q_LORA:MLA_Q_LORA + MLA_KV_LORA], g_kv)
    cos, sin = rope_tables(q_pos)
    k_rope = apply_rope(a[..., MLA_Q_LORA + MLA_KV_LORA:], cos, sin)
    q = (c_q @ w_qb).reshape(B, S, MLA_HEADS, MLA_NOPE + MLA_ROPE)
    q_nope = q[..., :MLA_NOPE]
    q_rope = apply_rope(q[..., MLA_NOPE:], cos[:, None, :], sin[:, None, :])
    if past_ckv is None:
        ckv_all, kr_all, k_pos = c_kv, k_rope, q_pos
    else:
        ckv_all = jnp.concatenate([past_ckv, c_kv], axis=1)
        kr_all = jnp.concatenate([past_krope, k_rope], axis=1)
        k_pos = jnp.arange(past_ckv.shape[1] + S, dtype=jnp.int32)
    T = ckv_all.shape[1]
    kv = (ckv_all @ w_kvb).reshape(B, T, MLA_HEADS, MLA_NOPE + MLA_V)
    k_nope, v = kv[..., :MLA_NOPE], kv[..., MLA_NOPE:]
    out = attend_queries(mla_attend, (q_nope, q_rope), q_pos, (k_nope, kr_all, v, k_pos))
    return out.reshape(B, S, MLA_HEADS * MLA_V) @ w_o, c_kv, k_rope


def sb_mixer(h, q_pos, past_k, past_v, w_in, w_o):
    B, S, _ = h.shape
    qkv = (h @ w_in).reshape(B, S, 3, SB_HEADS, SB_HEAD_DIM)
    q, k, v = qkv[:, :, 0], qkv[:, :, 1], qkv[:, :, 2]
    if past_k is None:
        k_all, v_all, k_pos = k, v, q_pos
    else:
        k_all = jnp.concatenate([past_k, k], axis=1)
        v_all = jnp.concatenate([past_v, v], axis=1)
        k_pos = jnp.arange(past_k.shape[1] + S, dtype=jnp.int32)
    out = attend_queries(sb_attend, (q,), q_pos, (k_all, v_all, k_pos))
    return out.reshape(B, S, SB_WIDTH) @ w_o, k, v


def run_trunk(x, q_pos, caches, weights):
    (norm_mix, norm_mlp, norm_final, mla_w_in, mla_g_q, mla_w_qb, mla_g_kv, mla_w_kvb, mla_w_o,
     sb_w_in, sb_w_o, mlp_w_up, mlp_w_down) = weights
    ckv_l, kr_l, k_l, v_l = [], [], [], []
    for i in range(DEPTH):
        j = i // N_MIXERS
        h = rmsnorm(x, norm_mix[i])
        if i % N_MIXERS == 0:
            pc = caches[0][j] if caches is not None else None
            pk = caches[1][j] if caches is not None else None
            out, c, kr = mla_mixer(h, q_pos, pc, pk, mla_w_in[j], mla_g_q[j], mla_w_qb[j],
                                   mla_g_kv[j], mla_w_kvb[j], mla_w_o[j])
            ckv_l.append(c)
            kr_l.append(kr)
        else:
            pk = caches[2][j] if caches is not None else None
            pv = caches[3][j] if caches is not None else None
            out, k, v = sb_mixer(h, q_pos, pk, pv, sb_w_in[j], sb_w_o[j])
            k_l.append(k)
            v_l.append(v)
        x = x + out
        h = rmsnorm(x, norm_mlp[i])
        x = x + jnp.square(jax.nn.relu(h @ mlp_w_up[i])) @ mlp_w_down[i]
    y = rmsnorm(x, norm_final)
    return y, jnp.stack(ckv_l), jnp.stack(kr_l), jnp.stack(k_l), jnp.stack(v_l)


def setup_inputs(seed: int = 0) -> dict:
    key = jax.random.key(seed)
    ks = jax.random.split(key, 24)
    f32 = jnp.float32

    def w(k, shape, fan_in):
        return jax.random.normal(k, shape, f32) * (fan_in ** -0.5)

    def gain(k, shape):
        return 1.0 + 0.1 * jax.random.normal(k, shape, f32)

    return {
        'x_prompt': jax.random.normal(ks[0], (BATCH, SEQ, D_MODEL), f32),
        'x_sample': jax.random.normal(ks[1], (DEC_BATCH, DEC_SEQ, D_MODEL), f32),
        'cache_mla_ckv': jax.random.normal(ks[2], (N_MLA, DEC_BATCH, PAST_LEN, MLA_KV_LORA), f32),
        'cache_mla_krope': jax.random.normal(ks[3], (N_MLA, DEC_BATCH, PAST_LEN, MLA_ROPE), f32),
        'cache_sb_k': jax.random.normal(ks[4], (N_SB, DEC_BATCH, PAST_LEN, SB_HEADS, SB_HEAD_DIM), f32),
        'cache_sb_v': jax.random.normal(ks[5], (N_SB, DEC_BATCH, PAST_LEN, SB_HEADS, SB_HEAD_DIM), f32),
        'norm_mix': gain(ks[6], (DEPTH, D_MODEL)),
        'norm_mlp': gain(ks[7], (DEPTH, D_MODEL)),
        'norm_final': gain(ks[8], (D_MODEL,)),
        'mla_w_in': w(ks[9], (N_MLA, D_MODEL, MLA_IN), D_MODEL),
        'mla_g_q': gain(ks[10], (N_MLA, MLA_Q_LORA)),
        'mla_w_qb': w(ks[11], (N_MLA, MLA_Q_LORA, MLA_HEADS * (MLA_NOPE + MLA_ROPE)), MLA_Q_LORA),
        'mla_g_kv': gain(ks[12], (N_MLA, MLA_KV_LORA)),
        'mla_w_kvb': w(ks[13], (N_MLA, MLA_KV_LORA, MLA_HEADS * (MLA_NOPE + MLA_V)), MLA_KV_LORA),
        'mla_w_o': w(ks[14], (N_MLA, MLA_HEADS * MLA_V, D_MODEL), MLA_HEADS * MLA_V),
        'sb_w_in': w(ks[15], (N_SB, D_MODEL, 3 * SB_WIDTH), D_MODEL),
        'sb_w_o': w(ks[16], (N_SB, SB_WIDTH, D_MODEL), SB_WIDTH),
        'mlp_w_up': w(ks[17], (DEPTH, D_MODEL, D_FF), D_MODEL),
        'mlp_w_down': w(ks[18], (DEPTH, D_FF, D_MODEL), D_FF),
    }


def reference(x_prompt, x_sample, cache_mla_ckv, cache_mla_krope, cache_sb_k, cache_sb_v,
              norm_mix, norm_mlp, norm_final, mla_w_in, mla_g_q, mla_w_qb, mla_g_kv, mla_w_kvb,
              mla_w_o, sb_w_in, sb_w_o, mlp_w_up, mlp_w_down):
    weights = (norm_mix, norm_mlp, norm_final, mla_w_in, mla_g_q, mla_w_qb, mla_g_kv, mla_w_kvb,
               mla_w_o, sb_w_in, sb_w_o, mlp_w_up, mlp_w_down)
    past = cache_mla_ckv.shape[2]
    s_new = x_sample.shape[1]
    p_pos = jnp.arange(x_prompt.shape[1], dtype=jnp.int32)
    s_pos = past + jnp.arange(s_new, dtype=jnp.int32)
    y_prompt, p_ckv, p_kr, p_k, p_v = run_trunk(x_prompt, p_pos, None, weights)
    y_sample, s_ckv, s_kr, s_k, s_v = run_trunk(
        x_sample, s_pos, (cache_mla_ckv, cache_mla_krope, cache_sb_k, cache_sb_v), weights)
    return (y_prompt, y_sample, p_ckv, p_kr, p_k, p_v, s_ckv, s_kr, s_k, s_v)
```

```python
import functools

import jax
import jax.numpy as jnp
from jax import lax
from jax.experimental import pallas as pl
from jax.experimental.pallas import tpu as pltpu

D_MODEL = 1024
DEPTH = 4
CHUNK = 64
MLA_HEADS = 8
MLA_NOPE = 128
MLA_ROPE = 64
MLA_V = 128
MLA_Q_LORA = 512
MLA_KV_LORA = 256
ROPE_THETA = 10000.0
SB_HEADS = 16
SB_HEAD_DIM = 64
SB_WIDTH = SB_HEADS * SB_HEAD_DIM
D_FF = 4 * D_MODEL
EPS = 1e-6

LANES = 128
MLA_IN_EXT = MLA_Q_LORA + MLA_KV_LORA + 2 * MLA_ROPE
MLA_QN = MLA_HEADS * MLA_NOPE
MLA_QR = MLA_HEADS * MLA_ROPE
VMEM_LIMIT_BYTES = 56 * 1024 * 1024

SB_LOG_ZERO = -104.0

F32 = jnp.float32
BF16 = jnp.bfloat16


def _rms(x, g):
    return x * lax.rsqrt(jnp.mean(x * x, axis=-1, keepdims=True) + EPS) * g


def _params(*sem):
    return pltpu.CompilerParams(dimension_semantics=sem, vmem_limit_bytes=VMEM_LIMIT_BYTES)


def _resident(shape):
    nd = len(shape)
    return pl.BlockSpec(shape, lambda *_: (0,) * nd, pipeline_mode=pl.Buffered(1))


def _row_block(n):
    for tm in (512, 256, 128, 64, 32, 16, 8):
        if n % tm == 0:
            return tm
    raise ValueError(f"row count {n} is not a multiple of 8")


def _mla_proj_kernel(x_ref, gmix_ref, win_ref, gq_ref, gkv_ref, wqb_ref, cos_ref, sin_ref,
                     ckv_ref, krope_ref, qn_ref, qr_ref):
    scale = (MLA_NOPE + MLA_ROPE) ** -0.5
    h = _rms(x_ref[...], gmix_ref[...]).astype(BF16)
    a = jnp.dot(h, win_ref[...], preferred_element_type=F32)
    c_q = _rms(a[:, :MLA_Q_LORA], gq_ref[...])
    c_kv = _rms(a[:, MLA_Q_LORA:MLA_Q_LORA + MLA_KV_LORA], gkv_ref[...])
    ckv_ref[...] = c_kv
    cos = cos_ref[...]
    sin = sin_ref[...]
    kr_pair = a[:, MLA_Q_LORA + MLA_KV_LORA:]
    lane = lax.broadcasted_iota(jnp.int32, kr_pair.shape, 1)
    u = kr_pair * jnp.where(lane < MLA_ROPE, cos, sin)
    krope_ref[...] = (u + pltpu.roll(u, MLA_ROPE, axis=1))[:, :MLA_ROPE]
    q = jnp.dot(c_q.astype(BF16), wqb_ref[...], preferred_element_type=F32)
    qn_ref[...] = (q[:, :MLA_QN] * scale).astype(BF16)
    cos4 = jnp.concatenate([cos] * (MLA_QR // LANES), axis=1)
    sin4 = jnp.concatenate([sin] * (MLA_QR // LANES), axis=1)
    qr = q[:, MLA_QN:MLA_QN + MLA_QR] * cos4 + q[:, MLA_QN + MLA_QR:] * sin4
    qr_ref[...] = (qr * scale).astype(BF16)


def _mla_proj(x, g_mix, w_in_ext, g_q, g_kv, w_qb_ext, cos_t, sin_t):
    n = x.shape[0]
    tm = min(_row_block(n), _row_block(cos_t.shape[0]))
    n_tab = cos_t.shape[0] // tm
    row = lambda w: pl.BlockSpec((tm, w), lambda i: (i, 0))
    tab = pl.BlockSpec((tm, LANES), lambda i: (i % n_tab, 0))
    return pl.pallas_call(
        _mla_proj_kernel,
        grid=(n // tm,),
        in_specs=[row(D_MODEL), _resident(g_mix.shape), _resident(w_in_ext.shape),
                  _resident(g_q.shape), _resident(g_kv.shape), _resident(w_qb_ext.shape),
                  tab, tab],
        out_specs=[row(MLA_KV_LORA), row(MLA_ROPE), row(MLA_QN), row(MLA_QR)],
        out_shape=[jax.ShapeDtypeStruct((n, MLA_KV_LORA), F32),
                   jax.ShapeDtypeStruct((n, MLA_ROPE), F32),
                   jax.ShapeDtypeStruct((n, MLA_QN), BF16),
                   jax.ShapeDtypeStruct((n, MLA_QR), BF16)],
        compiler_params=_params("parallel"),
        name="mla_proj",
    )(x, g_mix, w_in_ext, g_q, g_kv, w_qb_ext, cos_t, sin_t)


def _mla_kv_kernel(ckv_ref, w_ref, kn_ref, v_ref):
    kv = jnp.dot(ckv_ref[...].astype(BF16), w_ref[...], preferred_element_type=F32)
    kn_ref[...] = kv[:, :MLA_QN].astype(BF16)
    v_ref[...] = kv[:, MLA_QN:].astype(BF16)


def _mla_kv(ckv, w_kv_ext):
    n = ckv.shape[0]
    tm = _row_block(n)
    row = lambda w: pl.BlockSpec((tm, w), lambda i: (i, 0))
    return pl.pallas_call(
        _mla_kv_kernel,
        grid=(n // tm,),
        in_specs=[row(MLA_KV_LORA), _resident(w_kv_ext.shape)],
        out_specs=[row(MLA_QN), row(MLA_HEADS * MLA_V)],
        out_shape=[jax.ShapeDtypeStruct((n, MLA_QN), BF16),
                   jax.ShapeDtypeStruct((n, MLA_HEADS * MLA_V), BF16)],
        compiler_params=_params("parallel"),
        name="mla_kv",
    )(ckv, w_kv_ext)


def _mla_attn_kernel(qn_ref, qr_ref, kn_ref, kr_ref, v_ref, o_ref, *, tq, tk, q_off, t_valid):
    p0 = q_off + pl.program_id(2) * tq
    n_full = p0 // tk
    q = jnp.concatenate([qn_ref[0], qr_ref[0]], axis=1)

    def block(kb, carry, masked):
        m, l, acc = carry
        start = pl.multiple_of(kb * tk, tk)
        k = jnp.concatenate([kn_ref[0, pl.ds(start, tk), :], kr_ref[0, 0, pl.ds(start, tk), :]],
                            axis=1)
        s = lax.dot_general(q, k, (((1,), (1,)), ((), ())), preferred_element_type=F32)
        if masked:
            q_pos = p0 + lax.broadcasted_iota(jnp.int32, (tq, tk), 0)
            k_pos = start + lax.broadcasted_iota(jnp.int32, (tq, tk), 1)
            ok = (k_pos // CHUNK <= q_pos // CHUNK) & (k_pos < t_valid)
            s = jnp.where(ok, s, -jnp.inf)
        m_new = jnp.maximum(m, jnp.max(s, axis=1, keepdims=True))
        alpha = jnp.exp(m - m_new)
        p = jnp.exp(s - m_new)
        l = alpha * l + jnp.sum(p, axis=1, keepdims=True)
        acc = alpha * acc + jnp.dot(p.astype(BF16), v_ref[0, pl.ds(start, tk), :],
                                    preferred_element_type=F32)
        return m_new, l, acc

    init = (jnp.full((tq, 1), -1e30, F32), jnp.zeros((tq, 1), F32), jnp.zeros((tq, MLA_V), F32))
    carry = lax.fori_loop(0, n_full, lambda kb, c: block(kb, c, False), init)
    _, l, acc = block(n_full, carry, True)
    o_ref[0] = (acc / l).astype(o_ref.dtype)


def _mla_attn(qn, qr, kn, kr2, v, *, q_off, t_valid, tq, tk):
    b, sq, _ = qn.shape
    t = kn.shape[1]
    assert sq % tq == 0 and t % tk == 0 and q_off % tk == 0 and tk % tq == 0
    assert sq == tq or tq == tk
    kern = functools.partial(_mla_attn_kernel, tq=tq, tk=tk, q_off=q_off, t_valid=t_valid)
    return pl.pallas_call(
        kern,
        grid=(b, MLA_HEADS, sq // tq),
        in_specs=[pl.BlockSpec((1, tq, LANES), lambda bi, h, qi: (bi, qi, h)),
                  pl.BlockSpec((1, tq, LANES), lambda bi, h, qi: (bi, qi, h // 2)),
                  pl.BlockSpec((1, t, LANES), lambda bi, h, qi: (bi, 0, h)),
                  pl.BlockSpec((1, 1, t, LANES), lambda bi, h, qi: (bi, h % 2, 0, 0)),
                  pl.BlockSpec((1, t, LANES), lambda bi, h, qi: (bi, 0, h))],
        out_specs=pl.BlockSpec((1, tq, LANES), lambda bi, h, qi: (bi, qi, h)),
        out_shape=jax.ShapeDtypeStruct((b, sq, MLA_HEADS * MLA_V), BF16),
        compiler_params=_params("parallel", "parallel", "arbitrary"),
        name="mla_attn",
    )(qn, qr, kn, kr2, v)


def _sb_proj_kernel(x_ref, g_ref, w_ref, q_ref, k_ref, v_ref, kb_ref, vb_ref):
    h = _rms(x_ref[...], g_ref[...]).astype(BF16)
    qkv = jnp.dot(h, w_ref[...], preferred_element_type=F32)
    q_ref[...] = (qkv[:, :SB_WIDTH] * (SB_HEAD_DIM ** -0.5)).astype(BF16)
    k = qkv[:, SB_WIDTH:2 * SB_WIDTH]
    v = qkv[:, 2 * SB_WIDTH:]
    k_ref[...] = k
    v_ref[...] = v
    kb_ref[...] = k.astype(BF16)
    vb_ref[...] = v.astype(BF16)


def _sb_proj(x, g, w_in):
    n = x.shape[0]
    tm = _row_block(n)
    row = pl.BlockSpec((tm, SB_WIDTH), lambda i: (i, 0))
    return pl.pallas_call(
        _sb_proj_kernel,
        grid=(n // tm,),
        in_specs=[pl.BlockSpec((tm, D_MODEL), lambda i: (i, 0)), _resident(g.shape),
                  _resident(w_in.shape)],
        out_specs=[row] * 5,
        out_shape=[jax.ShapeDtypeStruct((n, SB_WIDTH), BF16),
                   jax.ShapeDtypeStruct((n, SB_WIDTH), F32),
                   jax.ShapeDtypeStruct((n, SB_WIDTH), F32),
                   jax.ShapeDtypeStruct((n, SB_WIDTH), BF16),
                   jax.ShapeDtypeStruct((n, SB_WIDTH), BF16)],
        compiler_params=_params("parallel"),
        name="sb_proj",
    )(x, g, w_in)


def _sb_attn_kernel(q_ref, k_ref, v_ref, o_ref, *, tq, tk, q_off):
    p0 = q_off + pl.program_id(2) * tq
    first = p0 // tk
    q2 = q_ref[0]
    head_lane = lax.broadcasted_iota(jnp.int32, (tq, LANES), 1) // SB_HEAD_DIM
    later = (lax.broadcasted_iota(jnp.int32, (tk, tk), 0)
             > lax.broadcasted_iota(jnp.int32, (tk, tk), 1)).astype(BF16)

    def block(qh, kb, carry, acc, masked):
        start = pl.multiple_of(kb * tk, tk)
        z = lax.dot_general(qh, k_ref[0, pl.ds(start, tk), :], (((1,), (1,)), ((), ())),
                            preferred_element_type=F32)
        softplus = jnp.maximum(z, 0.0) + jnp.log(1.0 + jnp.exp(-jnp.abs(z)))
        log_keep = -softplus
        if masked:
            q_pos = p0 + lax.broadcasted_iota(jnp.int32, (tq, tk), 0)
            k_pos = start + lax.broadcasted_iota(jnp.int32, (tq, tk), 1)
            ok = k_pos < q_pos
            log_keep = jnp.where(ok, log_keep, 0.0)
        hi = log_keep.astype(BF16)
        lo = (log_keep - hi.astype(F32)).astype(BF16)
        suffix = (jnp.dot(hi, later, preferred_element_type=F32)
                  + jnp.dot(lo, later, preferred_element_type=F32) + carry)
        a = jnp.exp(z - softplus + suffix)
        if masked:
            a = jnp.where(ok, a, 0.0)
        acc = acc + jnp.dot(a.astype(BF16), v_ref[0, pl.ds(start, tk), :],
                            preferred_element_type=F32)
        carry = carry + jnp.sum(log_keep, axis=1, keepdims=True)
        return carry, acc

    def one_head(hh):
        qh = jnp.where(head_lane == hh, q2, jnp.zeros_like(q2))
        carry, acc = block(qh, first, jnp.zeros((tq, 1), F32), jnp.zeros((tq, LANES), F32), True)

        def cond(state):
            kb, go, _, _ = state
            return jnp.logical_and(kb >= 0, go)

        def body(state):
            kb, _, carry, acc = state
            carry, acc = block(qh, kb, carry, acc, False)
            return kb - 1, jnp.max(carry) > SB_LOG_ZERO, carry, acc

        _, _, _, acc = lax.while_loop(cond, body, (first - 1, jnp.max(carry) > SB_LOG_ZERO,
                                                   carry, acc))
        return acc

    acc0 = one_head(0)
    acc1 = one_head(1)
    o_ref[0] = jnp.where(head_lane == 0, acc0, acc1).astype(o_ref.dtype)


def _sb_attn(q, k, v, *, q_off, tq, tk):
    b, sq, _ = q.shape
    t = k.shape[1]
    assert sq % tq == 0 and t % tk == 0 and q_off % tk == 0 and tk % tq == 0
    assert sq == tq or tq == tk
    kern = functools.partial(_sb_attn_kernel, tq=tq, tk=tk, q_off=q_off)
    pairs = SB_WIDTH // LANES
    return pl.pallas_call(
        kern,
        grid=(b, pairs, sq // tq),
        in_specs=[pl.BlockSpec((1, tq, LANES), lambda bi, h, qi: (bi, qi, h)),
                  pl.BlockSpec((1, t, LANES), lambda bi, h, qi: (bi, 0, h)),
                  pl.BlockSpec((1, t, LANES), lambda bi, h, qi: (bi, 0, h))],
        out_specs=pl.BlockSpec((1, tq, LANES), lambda bi, h, qi: (bi, qi, h)),
        out_shape=jax.ShapeDtypeStruct((b, sq, SB_WIDTH), BF16),
        compiler_params=_params("parallel", "parallel", "arbitrary"),
        name="sb_attn",
    )(q, k, v)


def _post_kernel(x_ref, o_ref, wo_ref, g_ref, wup_ref, wdn_ref, gfin_ref, y_ref, *, final, ff_blk):
    x1 = x_ref[...] + jnp.dot(o_ref[...], wo_ref[...], preferred_element_type=F32)
    h = _rms(x1, g_ref[...]).astype(BF16)
    acc = x1
    for c in range(0, D_FF, ff_blk):
        up = jnp.dot(h, wup_ref[:, c:c + ff_blk], preferred_element_type=F32)
        act = jnp.square(jnp.maximum(up, 0.0)).astype(BF16)
        acc = acc + jnp.dot(act, wdn_ref[c:c + ff_blk, :], preferred_element_type=F32)
    y_ref[...] = _rms(acc, gfin_ref[...]) if final else acc


def _post(x, o, w_o, g_mlp, w_up, w_dn, g_fin, *, final):
    n = x.shape[0]
    tm = _row_block(n)
    row = pl.BlockSpec((tm, D_MODEL), lambda i: (i, 0))
    kern = functools.partial(_post_kernel, final=final, ff_blk=1024)
    return pl.pallas_call(
        kern,
        grid=(n // tm,),
        in_specs=[row, row, _resident(w_o.shape), _resident(g_mlp.shape), _resident(w_up.shape),
                  _resident(w_dn.shape), _resident(g_fin.shape)],
        out_specs=row,
        out_shape=jax.ShapeDtypeStruct((n, D_MODEL), F32),
        compiler_params=_params("parallel"),
        name="post_final" if final else "post",
    )(x, o, w_o, g_mlp, w_up, w_dn, g_fin)


def _prep_weights(norm_mix, norm_mlp, norm_final, mla_w_in, mla_g_q, mla_w_qb, mla_g_kv,
                  mla_w_kvb, mla_w_o, sb_w_in, sb_w_o, mlp_w_up, mlp_w_down):
    half = MLA_ROPE // 2
    lat = MLA_Q_LORA + MLA_KV_LORA
    w_in_ext = jnp.concatenate(
        [mla_w_in, -mla_w_in[:, :, lat + half:], mla_w_in[:, :, lat:lat + half]], axis=2)
    n_mla = mla_w_qb.shape[0]
    qb = mla_w_qb.reshape(n_mla, MLA_Q_LORA, MLA_HEADS, MLA_NOPE + MLA_ROPE)
    qb_rot = jnp.concatenate([-qb[..., MLA_NOPE + half:], qb[..., MLA_NOPE:MLA_NOPE + half]], axis=-1)
    w_qb_ext = jnp.concatenate(
        [qb[..., :MLA_NOPE].reshape(n_mla, MLA_Q_LORA, MLA_QN),
         qb[..., MLA_NOPE:].reshape(n_mla, MLA_Q_LORA, MLA_QR),
         qb_rot.reshape(n_mla, MLA_Q_LORA, MLA_QR)], axis=2)
    kvb = mla_w_kvb.reshape(n_mla, MLA_KV_LORA, MLA_HEADS, MLA_NOPE + MLA_V)
    w_kv_ext = jnp.concatenate(
        [kvb[..., :MLA_NOPE].reshape(n_mla, MLA_KV_LORA, MLA_QN),
         kvb[..., MLA_NOPE:].reshape(n_mla, MLA_KV_LORA, MLA_HEADS * MLA_V)], axis=2)
    return dict(
        norm_mix=norm_mix[:, None, :], norm_mlp=norm_mlp[:, None, :], norm_final=norm_final[None, :],
        w_in_ext=w_in_ext.astype(BF16), g_q=mla_g_q[:, None, :], g_kv=mla_g_kv[:, None, :],
        w_qb_ext=w_qb_ext.astype(BF16), w_kv_ext=w_kv_ext.astype(BF16),
        mla_w_o=mla_w_o.astype(BF16), sb_w_in=sb_w_in.astype(BF16), sb_w_o=sb_w_o.astype(BF16),
        w_up=mlp_w_up.astype(BF16), w_dn=mlp_w_down.astype(BF16))


def _rope_tables(pos, reps):
    inv = ROPE_THETA ** (-jnp.arange(0, MLA_ROPE, 2, dtype=F32) / MLA_ROPE)
    ang = pos.astype(F32)[:, None] * inv[None, :]
    tile = lambda t: jnp.tile(t, (reps, LANES // (MLA_ROPE // 2)))
    return tile(jnp.cos(ang)), tile(jnp.sin(ang))


def _pad_keys(past, new, t_pad):
    parts = [new.astype(BF16)] if past is None else [past.astype(BF16), new.astype(BF16)]
    rows = sum(p.shape[1] for p in parts)
    if t_pad > rows:
        parts.append(jnp.zeros((new.shape[0], t_pad - rows) + new.shape[2:], BF16))
    return parts[0] if len(parts) == 1 else jnp.concatenate(parts, axis=1)


def _trunk(x, caches, w):
    b, s, _ = x.shape
    past = 0 if caches is None else caches[0].shape[2]
    if s >= 256:
        tq = tk = 256
    else:
        tq, tk = s, LANES
    t_valid = past + s
    t_pad = -(-t_valid // tk) * tk
    n = b * s
    pos = past + jnp.arange(s, dtype=jnp.int32)
    cos_t, sin_t = _rope_tables(pos, 1 if s >= 512 else b)
    xf = x.reshape(n, D_MODEL)
    ckv_l, kr_l, k_l, v_l = [], [], [], []
    for i in range(DEPTH):
        j = i // 2
        if i % 2 == 0:
            ckv, krope, qn, qr = _mla_proj(xf, w["norm_mix"][i], w["w_in_ext"][j], w["g_q"][j],
                                           w["g_kv"][j], w["w_qb_ext"][j], cos_t, sin_t)
            ckv = ckv.reshape(b, s, MLA_KV_LORA)
            krope = krope.reshape(b, s, MLA_ROPE)
            ckv_l.append(ckv)
            kr_l.append(krope)
            ckv_all = ckv if caches is None else jnp.concatenate([caches[0][j], ckv], axis=1)
            kn, v = _mla_kv(ckv_all.reshape(-1, MLA_KV_LORA), w["w_kv_ext"][j])
            kn = _pad_keys(None, kn.reshape(b, t_valid, MLA_QN), t_pad)
            v = _pad_keys(None, v.reshape(b, t_valid, MLA_HEADS * MLA_V), t_pad)
            kr = _pad_keys(None if caches is None else caches[1][j], krope, t_pad)
            zero = jnp.zeros_like(kr)
            kr2 = jnp.stack([jnp.concatenate([kr, zero], axis=2),
                             jnp.concatenate([zero, kr], axis=2)], axis=1)
            o = _mla_attn(qn.reshape(b, s, MLA_QN), qr.reshape(b, s, MLA_QR), kn, kr2, v,
                          q_off=past, t_valid=t_valid, tq=tq, tk=tk)
            w_o = w["mla_w_o"][j]
        else:
            q, k, v, kb, vb = _sb_proj(xf, w["norm_mix"][i], w["sb_w_in"][j])
            k_l.append(k.reshape(b, s, SB_HEADS, SB_HEAD_DIM))
            v_l.append(v.reshape(b, s, SB_HEADS, SB_HEAD_DIM))
            flat = lambda c: c[j].reshape(b, past, SB_WIDTH)
            k_all = _pad_keys(None if caches is None else flat(caches[2]), kb.reshape(b, s, SB_WIDTH), t_pad)
            v_all = _pad_keys(None if caches is None else flat(caches[3]), vb.reshape(b, s, SB_WIDTH), t_pad)
            o = _sb_attn(q.reshape(b, s, SB_WIDTH), k_all, v_all, q_off=past, tq=tq, tk=tk)
            w_o = w["sb_w_o"][j]
        xf = _post(xf, o.reshape(n, D_MODEL), w_o, w["norm_mlp"][i], w["w_up"][i], w["w_dn"][i],
                   w["norm_final"], final=(i == DEPTH - 1))
    return (xf.reshape(b, s, D_MODEL), jnp.stack(ckv_l), jnp.stack(kr_l), jnp.stack(k_l),
            jnp.stack(v_l))


def kernel(x_prompt, x_sample, cache_mla_ckv, cache_mla_krope, cache_sb_k, cache_sb_v, norm_mix, norm_mlp, norm_final, mla_w_in, mla_g_q, mla_w_qb, mla_g_kv, mla_w_kvb, mla_w_o, sb_w_in, sb_w_o, mlp_w_up, mlp_w_down):
    w = _prep_weights(norm_mix, norm_mlp, norm_final, mla_w_in, mla_g_q, mla_w_qb, mla_g_kv,
                      mla_w_kvb, mla_w_o, sb_w_in, sb_w_o, mlp_w_up, mlp_w_down)
    y_p, p_ckv, p_kr, p_k, p_v = _trunk(x_prompt, None, w)
    y_s, s_ckv, s_kr, s_k, s_v = _trunk(
        x_sample, (cache_mla_ckv, cache_mla_krope, cache_sb_k, cache_sb_v), w)
    return (y_p, y_s, p_ckv, p_kr, p_k, p_v, s_ckv, s_kr, s_k, s_v)
```

```python
import functools
import math

import jax
import jax.numpy as jnp
from jax import lax
from jax.experimental import pallas as pl
from jax.experimental.pallas import tpu as pltpu

D_MODEL = 1024
DEPTH = 4
CHUNK = 64
MLA_HEADS = 8
MLA_NOPE = 128
MLA_ROPE = 64
MLA_V = 128
MLA_Q_LORA = 512
MLA_KV_LORA = 256
ROPE_THETA = 10000.0
SB_HEADS = 16
SB_HEAD_DIM = 64
SB_WIDTH = SB_HEADS * SB_HEAD_DIM
D_FF = 4 * D_MODEL
EPS = 1e-6

LANES = 128
MLA_IN_EXT = MLA_Q_LORA + MLA_KV_LORA + 2 * MLA_ROPE
MLA_QN = MLA_HEADS * MLA_NOPE
MLA_QR = MLA_HEADS * MLA_ROPE
MLA_VW = MLA_HEADS * MLA_V
VMEM_LIMIT_BYTES = 56 * 1024 * 1024
MLA_TILE = 512
SB_TILE = 256
LOG2E = math.log2(math.e)
LN2 = math.log(2.0)

SB_LOG2_ZERO = 151.0

F32 = jnp.float32
BF16 = jnp.bfloat16
NT = (((1,), (1,)), ((), ()))


def _rms(x, g):
    return x * lax.rsqrt(jnp.mean(x * x, axis=-1, keepdims=True) + EPS) * g


def _params(*sem):
    return pltpu.CompilerParams(dimension_semantics=sem, vmem_limit_bytes=VMEM_LIMIT_BYTES)


def _resident(shape):
    nd = len(shape)
    return pl.BlockSpec(shape, lambda *_: (0,) * nd, pipeline_mode=pl.Buffered(1))


def _row_block(n, cap=512):
    for tm in (512, 256, 128, 64, 32, 16, 8):
        if tm <= cap and n % tm == 0:
            return tm
    raise ValueError(f"row count {n} is not a multiple of 8")


def _softplus2(z2):
    return jnp.maximum(z2, 0.0) + jnp.log(1.0 + jnp.exp2(-jnp.abs(z2))) * (1.0 / LN2)


def _mla_proj_kernel(x_ref, gmix_ref, win_ref, gq_ref, gkv_ref, wqb_ref, cos_ref, sin_ref,
                     ckv_ref, krope_ref, qn_ref, qr_ref):
    scale = (MLA_NOPE + MLA_ROPE) ** -0.5 * LOG2E
    h = _rms(x_ref[...], gmix_ref[...]).astype(BF16)
    a = jnp.dot(h, win_ref[...], preferred_element_type=F32)
    c_q = _rms(a[:, :MLA_Q_LORA], gq_ref[...])
    c_kv = _rms(a[:, MLA_Q_LORA:MLA_Q_LORA + MLA_KV_LORA], gkv_ref[...])
    ckv_ref[...] = c_kv
    cos = cos_ref[...]
    sin = sin_ref[...]
    kr_pair = a[:, MLA_Q_LORA + MLA_KV_LORA:]
    lane = lax.broadcasted_iota(jnp.int32, kr_pair.shape, 1)
    u = kr_pair * jnp.where(lane < MLA_ROPE, cos, sin)
    krope_ref[...] = (u + pltpu.roll(u, MLA_ROPE, axis=1))[:, :MLA_ROPE]
    q = jnp.dot(c_q.astype(BF16), wqb_ref[...], preferred_element_type=F32)
    qn_ref[...] = (q[:, :MLA_QN] * scale).astype(BF16)
    cos4 = jnp.concatenate([cos] * (MLA_QR // LANES), axis=1)
    sin4 = jnp.concatenate([sin] * (MLA_QR // LANES), axis=1)
    qr = q[:, MLA_QN:MLA_QN + MLA_QR] * cos4 + q[:, MLA_QN + MLA_QR:] * sin4
    qr_ref[...] = (qr * scale).astype(BF16)


def _mla_proj(x, g_mix, w_in_ext, g_q, g_kv, w_qb_ext, cos_t, sin_t):
    n = x.shape[0]
    tm = min(_row_block(n), _row_block(cos_t.shape[0]))
    n_tab = cos_t.shape[0] // tm
    row = lambda w: pl.BlockSpec((tm, w), lambda i: (i, 0))
    tab = pl.BlockSpec((tm, LANES), lambda i: (i % n_tab, 0))
    return pl.pallas_call(
        _mla_proj_kernel,
        grid=(n // tm,),
        in_specs=[row(D_MODEL), _resident(g_mix.shape), _resident(w_in_ext.shape),
                  _resident(g_q.shape), _resident(g_kv.shape), _resident(w_qb_ext.shape),
                  tab, tab],
        out_specs=[row(MLA_KV_LORA), row(MLA_ROPE), row(MLA_QN), row(MLA_QR)],
        out_shape=[jax.ShapeDtypeStruct((n, MLA_KV_LORA), F32),
                   jax.ShapeDtypeStruct((n, MLA_ROPE), F32),
                   jax.ShapeDtypeStruct((n, MLA_QN), BF16),
                   jax.ShapeDtypeStruct((n, MLA_QR), BF16)],
        compiler_params=_params("parallel"),
        name="mla_proj",
    )(x, g_mix, w_in_ext, g_q, g_kv, w_qb_ext, cos_t, sin_t)


def _mla_kv_kernel(ckv_ref, wk_ref, wv_ref, kn_ref, v_ref, *, transposed):
    c = ckv_ref[...].astype(BF16)
    kn_ref[...] = jnp.dot(c, wk_ref[...], preferred_element_type=F32).astype(BF16)
    if transposed:
        v_ref[0, 0] = lax.dot_general(wv_ref[...], c, NT, preferred_element_type=F32).astype(BF16)
    else:
        v_ref[...] = jnp.dot(c, wv_ref[...], preferred_element_type=F32).astype(BF16)


def _mla_kv(ckv, w_k, w_v, *, tm, blocks_per_batch=None):
    n = ckv.shape[0]
    transposed = blocks_per_batch is not None
    row = lambda w: pl.BlockSpec((tm, w), lambda i: (i, 0))
    if transposed:
        nkb = blocks_per_batch
        v_spec = pl.BlockSpec((1, 1, MLA_VW, tm), lambda i: (i // nkb, i % nkb, 0, 0))
        v_shape = jax.ShapeDtypeStruct((n // (tm * nkb), nkb, MLA_VW, tm), BF16)
    else:
        v_spec, v_shape = row(MLA_VW), jax.ShapeDtypeStruct((n, MLA_VW), BF16)
    return pl.pallas_call(
        functools.partial(_mla_kv_kernel, transposed=transposed),
        grid=(n // tm,),
        in_specs=[row(MLA_KV_LORA), _resident(w_k.shape), _resident(w_v.shape)],
        out_specs=[row(MLA_QN), v_spec],
        out_shape=[jax.ShapeDtypeStruct((n, MLA_QN), BF16), v_shape],
        compiler_params=_params("parallel"),
        name="mla_kv_t" if transposed else "mla_kv",
    )(ckv, w_k, w_v)


def _mla_attn_t_kernel(qn_ref, qr_ref, kn_ref, kr_ref, vt_ref, o_ref, acc_ref, *, tq, tk):
    qi = pl.program_id(2)
    p0 = qi * tq
    qr = qr_ref[0]
    qs = [jnp.concatenate([qn_ref[0, :, hh * LANES:(hh + 1) * LANES], qr], axis=1)
          for hh in range(2)]
    acc_ref[...] = jnp.zeros_like(acc_ref)

    def scores(kb, hh):
        start = pl.multiple_of(kb * tk, tk)
        k = jnp.concatenate([kn_ref[0, pl.ds(start, tk), hh * LANES:(hh + 1) * LANES],
                             kr_ref[0, hh, pl.ds(start, tk), :]], axis=1)
        return lax.dot_general(k, qs[hh], NT, preferred_element_type=F32)

    def consume(kb, hh, st, m, l):
        m_new = jnp.maximum(m, jnp.max(st, axis=0, keepdims=True))
        alpha = jnp.exp2(m - m_new)
        pt = jnp.exp2(st - m_new)
        l = alpha * l + jnp.sum(pt, axis=0, keepdims=True)
        pv = jnp.dot(vt_ref[0, kb, hh * MLA_V:(hh + 1) * MLA_V, :], pt.astype(BF16),
                     preferred_element_type=F32)
        acc_ref[hh] = alpha * acc_ref[hh] + pv
        return m_new, l

    def step(kb, carry):
        return tuple(consume(kb, hh, scores(kb, hh), *carry[hh]) for hh in range(2))

    def two_steps(j, carry):
        st = [[scores(2 * j + i, hh) for hh in range(2)] for i in range(2)]
        for i in range(2):
            carry = tuple(consume(2 * j + i, hh, st[i][hh], *carry[hh]) for hh in range(2))
        return carry

    n_full = p0 // tk
    init = (jnp.full((1, tq), -1e30, F32), jnp.zeros((1, tq), F32))
    carry = lax.fori_loop(0, n_full // 2, two_steps, (init, init))
    carry = lax.fori_loop(2 * (n_full // 2), n_full, step, carry)
    k_pos = p0 + lax.broadcasted_iota(jnp.int32, (tk, tq), 0)
    q_pos = p0 + lax.broadcasted_iota(jnp.int32, (tk, tq), 1)
    visible = k_pos // CHUNK <= q_pos // CHUNK
    for hh in range(2):
        m, l = carry[hh]
        st = jnp.where(visible, scores(n_full, hh), -jnp.inf)
        _, l = consume(n_full, hh, st, m, l)
        o = acc_ref[hh] * (1.0 / l)
        o_ref[0, :, hh * MLA_V:(hh + 1) * MLA_V] = o.T.astype(o_ref.dtype)


def _mla_attn_t(qn, qr, kn, kr2, vt, *, tq, tk):
    b, s, _ = qn.shape
    nkb = vt.shape[1]
    assert tq == tk and s % tq == 0 and kn.shape[1] == s and nkb * tk == s
    kern = functools.partial(_mla_attn_t_kernel, tq=tq, tk=tk)
    return pl.pallas_call(
        kern,
        grid=(b, MLA_HEADS // 2, s // tq),
        in_specs=[pl.BlockSpec((1, tq, 2 * LANES), lambda bi, hp, qi: (bi, qi, hp)),
                  pl.BlockSpec((1, tq, LANES), lambda bi, hp, qi: (bi, qi, hp)),
                  pl.BlockSpec((1, s, 2 * LANES), lambda bi, hp, qi: (bi, 0, hp)),
                  pl.BlockSpec((1, 2, s, LANES), lambda bi, hp, qi: (bi, 0, 0, 0)),
                  pl.BlockSpec((1, nkb, 2 * MLA_V, tk), lambda bi, hp, qi: (bi, 0, hp, 0))],
        out_specs=pl.BlockSpec((1, tq, 2 * MLA_V), lambda bi, hp, qi: (bi, qi, hp)),
        out_shape=jax.ShapeDtypeStruct((b, s, MLA_VW), BF16),
        scratch_shapes=[pltpu.VMEM((2, MLA_V, tq), F32)],
        compiler_params=_params("parallel", "parallel", "arbitrary"),
        name="mla_attn_t",
    )(qn, qr, kn, kr2, vt)


def _mla_attn_kernel(qn_ref, qr_ref, kn_ref, kr_ref, v_ref, o_ref, *, q_off, t_valid):
    sq = qn_ref.shape[1]
    t = kn_ref.shape[1]
    q_pos = q_off + lax.broadcasted_iota(jnp.int32, (sq, t), 0)
    k_pos = lax.broadcasted_iota(jnp.int32, (sq, t), 1)
    visible = (k_pos // CHUNK <= q_pos // CHUNK) & (k_pos < t_valid)
    for h in range(MLA_HEADS):
        cols = slice(h * LANES, (h + 1) * LANES)
        pair = slice((h // 2) * LANES, (h // 2 + 1) * LANES)
        q = jnp.concatenate([qn_ref[0, :, cols], qr_ref[0, :, pair]], axis=1)
        k = jnp.concatenate([kn_ref[0, :, cols], kr_ref[0, h % 2]], axis=1)
        s = jnp.where(visible, lax.dot_general(q, k, NT, preferred_element_type=F32), -jnp.inf)
        p = jnp.exp2(s - jnp.max(s, axis=1, keepdims=True))
        o = jnp.dot(p.astype(BF16), v_ref[0, :, cols], preferred_element_type=F32)
        o_ref[0, :, cols] = (o / jnp.sum(p, axis=1, keepdims=True)).astype(o_ref.dtype)


def _stream_block(*shape):
    return pl.BlockSpec((1,) + shape, lambda bi: (bi,) + (0,) * len(shape))


def _mla_attn(qn, qr, kn, kr2, v, *, q_off, t_valid):
    b, sq, _ = qn.shape
    t = kn.shape[1]
    kern = functools.partial(_mla_attn_kernel, q_off=q_off, t_valid=t_valid)
    return pl.pallas_call(
        kern,
        grid=(b,),
        in_specs=[_stream_block(sq, MLA_QN), _stream_block(sq, MLA_QR), _stream_block(t, MLA_QN),
                  _stream_block(2, t, LANES), _stream_block(t, MLA_VW)],
        out_specs=_stream_block(sq, MLA_VW),
        out_shape=jax.ShapeDtypeStruct((b, sq, MLA_VW), BF16),
        compiler_params=_params("parallel"),
        name="mla_attn",
    )(qn, qr, kn, kr2, v)


def _sb_proj_kernel(x_ref, g_ref, w_ref, wvt_ref, q_ref, k_ref, v_ref, kb_ref, vb_ref, *, transposed):
    h = _rms(x_ref[...], g_ref[...]).astype(BF16)
    qkv = jnp.dot(h, w_ref[...], preferred_element_type=F32)
    q_ref[...] = (qkv[:, :SB_WIDTH] * (SB_HEAD_DIM ** -0.5 * LOG2E)).astype(BF16)
    k = qkv[:, SB_WIDTH:2 * SB_WIDTH]
    v = qkv[:, 2 * SB_WIDTH:]
    k_ref[...] = k
    v_ref[...] = v
    kb_ref[...] = k.astype(BF16)
    if transposed:
        vb_ref[0, 0] = lax.dot_general(wvt_ref[...], h, NT, preferred_element_type=F32).astype(BF16)
    else:
        vb_ref[...] = v.astype(BF16)


def _sb_proj(x, g, w_in, w_vt, *, tm, blocks_per_batch=None):
    n = x.shape[0]
    transposed = blocks_per_batch is not None
    row = pl.BlockSpec((tm, SB_WIDTH), lambda i: (i, 0))
    if transposed:
        nkb = blocks_per_batch
        vb_spec = pl.BlockSpec((1, 1, SB_WIDTH, tm), lambda i: (i // nkb, i % nkb, 0, 0))
        vb_shape = jax.ShapeDtypeStruct((n // (tm * nkb), nkb, SB_WIDTH, tm), BF16)
    else:
        vb_spec, vb_shape = row, jax.ShapeDtypeStruct((n, SB_WIDTH), BF16)
    return pl.pallas_call(
        functools.partial(_sb_proj_kernel, transposed=transposed),
        grid=(n // tm,),
        in_specs=[pl.BlockSpec((tm, D_MODEL), lambda i: (i, 0)), _resident(g.shape),
                  _resident(w_in.shape), _resident(w_vt.shape)],
        out_specs=[row, row, row, row, vb_spec],
        out_shape=[jax.ShapeDtypeStruct((n, SB_WIDTH), BF16),
                   jax.ShapeDtypeStruct((n, SB_WIDTH), F32),
                   jax.ShapeDtypeStruct((n, SB_WIDTH), F32),
                   jax.ShapeDtypeStruct((n, SB_WIDTH), BF16),
                   vb_shape],
        compiler_params=_params("parallel"),
        name="sb_proj_t" if transposed else "sb_proj",
    )(x, g, w_in, w_vt)


def _sb_attn_t_kernel(q_ref, k_ref, vt_ref, o_ref, *, tq):
    tk = tq
    qi = pl.program_id(2)
    p0 = qi * tq
    prev = jnp.maximum(qi - 1, 0)
    has_prev = (qi > 0).astype(F32)
    q2 = q_ref[0]
    lane_head = lax.broadcasted_iota(jnp.int32, (tq, LANES), 1) // SB_HEAD_DIM
    qh = [jnp.where(lane_head == hh, q2, jnp.zeros_like(q2)) for hh in range(2)]
    key_row = lax.broadcasted_iota(jnp.int32, (tk, tk), 0)
    col = lax.broadcasted_iota(jnp.int32, (tk, tk), 1)
    later = (key_row < col).astype(BF16)
    causal = key_row < col

    def logits(kb, hh):
        start = pl.multiple_of(kb * tk, tk)
        return lax.dot_general(k_ref[0, pl.ds(start, tk), :], qh[hh], NT, preferred_element_type=F32)

    def suffix_in_block(sp):
        return jnp.dot(later, sp.astype(BF16), preferred_element_type=F32)

    def window(hh):
        z_diag = logits(qi, hh)
        z_prev = logits(prev, hh)
        sp_diag = _softplus2(z_diag)
        sp_prev = _softplus2(z_prev)
        spm = jnp.where(causal, sp_diag, 0.0)
        diag_sum = jnp.sum(spm, axis=0, keepdims=True)
        a_diag = jnp.where(causal, jnp.exp2(z_diag - sp_diag - suffix_in_block(spm)), 0.0)
        a_prev = jnp.exp2(z_prev - sp_prev - suffix_in_block(sp_prev) - diag_sum)
        acc = (jnp.dot(vt_ref[0, qi], a_diag.astype(BF16), preferred_element_type=F32)
               + has_prev * jnp.dot(vt_ref[0, prev], a_prev.astype(BF16),
                                    preferred_element_type=F32))
        return diag_sum + has_prev * jnp.sum(sp_prev, axis=0, keepdims=True), acc

    def earlier(kb, hh, carry, acc):
        zt = logits(kb, hh)
        sp = _softplus2(zt)
        a = jnp.exp2(zt - sp - suffix_in_block(sp) - carry).astype(BF16)
        acc = acc + jnp.dot(vt_ref[0, kb], a, preferred_element_type=F32)
        return carry + jnp.sum(sp, axis=0, keepdims=True), acc

    def unfinished(c0, c1):
        return jnp.minimum(jnp.min(c0), jnp.min(c1)) < SB_LOG2_ZERO

    c0, a0 = window(0)
    c1, a1 = window(1)

    def cond(state):
        return jnp.logical_and(state[0] >= 0, state[1])

    def body(state):
        kb, _, c0, c1, a0, a1 = state
        c0, a0 = earlier(kb, 0, c0, a0)
        c1, a1 = earlier(kb, 1, c1, a1)
        return kb - 1, unfinished(c0, c1), c0, c1, a0, a1

    _, _, _, _, a0, a1 = lax.while_loop(cond, body, (prev - 1, unfinished(c0, c1), c0, c1, a0, a1))
    row_head = lax.broadcasted_iota(jnp.int32, (LANES, tq), 0) // SB_HEAD_DIM
    o_ref[0] = jnp.where(row_head == 0, a0, a1).T.astype(o_ref.dtype)


def _sb_attn_t(q, k, vt, *, tq):
    b, s, _ = q.shape
    nkb = vt.shape[1]
    assert s % tq == 0 and nkb * tq == s and nkb >= 2 and vt.shape[3] == tq
    pairs = SB_WIDTH // LANES
    return pl.pallas_call(
        functools.partial(_sb_attn_t_kernel, tq=tq),
        grid=(b, pairs, s // tq),
        in_specs=[pl.BlockSpec((1, tq, LANES), lambda bi, h, qi: (bi, qi, h)),
                  pl.BlockSpec((1, s, LANES), lambda bi, h, qi: (bi, 0, h)),
                  pl.BlockSpec((1, nkb, LANES, tq), lambda bi, h, qi: (bi, 0, h, 0))],
        out_specs=pl.BlockSpec((1, tq, LANES), lambda bi, h, qi: (bi, qi, h)),
        out_shape=jax.ShapeDtypeStruct((b, s, SB_WIDTH), BF16),
        compiler_params=_params("parallel", "parallel", "arbitrary"),
        name="sb_attn_t",
    )(q, k, vt)


def _sb_attn_kernel(q_ref, k_ref, v_ref, o_ref, *, tk, q_off):
    sq = q_ref.shape[1]
    pairs = SB_WIDTH // LANES
    rows = 2 * pairs * sq
    first = q_off // tk
    lane_head = lax.broadcasted_iota(jnp.int32, (sq, LANES), 1) // SB_HEAD_DIM
    group = lambda p: slice(p * LANES, (p + 1) * LANES)
    lhs = []
    for p in range(pairs):
        q2 = q_ref[0, :, group(p)]
        lhs.append(jnp.concatenate([jnp.where(lane_head == hh, q2, jnp.zeros_like(q2))
                                    for hh in range(2)], axis=0))
    later = (lax.broadcasted_iota(jnp.int32, (tk, tk), 0)
             > lax.broadcasted_iota(jnp.int32, (tk, tk), 1)).astype(BF16)

    def block(kb, carry, acc, masked):
        start = pl.multiple_of(kb * tk, tk)
        z = jnp.concatenate(
            [lax.dot_general(lhs[p], k_ref[0, pl.ds(start, tk), group(p)], NT,
                             preferred_element_type=F32) for p in range(pairs)], axis=0)
        sp = _softplus2(z)
        spm = sp
        if masked:
            q_pos = q_off + lax.rem(lax.broadcasted_iota(jnp.int32, (rows, tk), 0), sq)
            k_pos = start + lax.broadcasted_iota(jnp.int32, (rows, tk), 1)
            ok = k_pos < q_pos
            spm = jnp.where(ok, sp, 0.0)
        suffix = jnp.dot(spm.astype(BF16), later, preferred_element_type=F32) + carry
        a = jnp.exp2(z - sp - suffix)
        if masked:
            a = jnp.where(ok, a, 0.0)
        a = a.astype(BF16)
        pv = jnp.concatenate(
            [jnp.dot(a[2 * sq * p:2 * sq * (p + 1)], v_ref[0, pl.ds(start, tk), group(p)],
                     preferred_element_type=F32) for p in range(pairs)], axis=0)
        return carry + jnp.sum(spm, axis=1, keepdims=True), acc + pv

    carry, acc = block(first, jnp.zeros((rows, 1), F32), jnp.zeros((rows, LANES), F32), True)

    def cond(state):
        return jnp.logical_and(state[0] >= 0, state[1])

    def body(state):
        kb, _, carry, acc = state
        carry, acc = block(kb, carry, acc, False)
        return kb - 1, jnp.min(carry) < SB_LOG2_ZERO, carry, acc

    acc = lax.while_loop(cond, body, (first - 1, jnp.min(carry) < SB_LOG2_ZERO, carry, acc))[3]
    for p in range(pairs):
        both = acc[2 * sq * p:2 * sq * (p + 1)]
        o_ref[0, :, group(p)] = jnp.where(lane_head == 0, both[:sq], both[sq:]).astype(o_ref.dtype)


def _sb_attn(q, k, v, *, q_off, tk):
    b, sq, _ = q.shape
    t = k.shape[1]
    assert t % tk == 0 and q_off % tk == 0 and sq <= tk and q_off + tk <= t
    kern = functools.partial(_sb_attn_kernel, tk=tk, q_off=q_off)
    return pl.pallas_call(
        kern,
        grid=(b,),
        in_specs=[_stream_block(sq, SB_WIDTH), _stream_block(t, SB_WIDTH), _stream_block(t, SB_WIDTH)],
        out_specs=_stream_block(sq, SB_WIDTH),
        out_shape=jax.ShapeDtypeStruct((b, sq, SB_WIDTH), BF16),
        compiler_params=_params("parallel"),
        name="sb_attn",
    )(q, k, v)


def _post_kernel(x_ref, o_ref, wo_ref, g_ref, wup_ref, wdn_ref, gfin_ref, y_ref, *, final, ff_blk):
    x1 = x_ref[...] + jnp.dot(o_ref[...], wo_ref[...], preferred_element_type=F32)
    h = _rms(x1, g_ref[...]).astype(BF16)
    acc = x1
    for c in range(0, D_FF, ff_blk):
        up = jnp.dot(h, wup_ref[:, c:c + ff_blk], preferred_element_type=F32)
        act = jnp.square(jnp.maximum(up, 0.0)).astype(BF16)
        acc = acc + jnp.dot(act, wdn_ref[c:c + ff_blk, :], preferred_element_type=F32)
    y_ref[...] = _rms(acc, gfin_ref[...]) if final else acc


def _post(x, o, w_o, g_mlp, w_up, w_dn, g_fin, *, final):
    n = x.shape[0]
    tm = _row_block(n)
    row = pl.BlockSpec((tm, D_MODEL), lambda i: (i, 0))
    kern = functools.partial(_post_kernel, final=final, ff_blk=1024)
    return pl.pallas_call(
        kern,
        grid=(n // tm,),
        in_specs=[row, row, _resident(w_o.shape), _resident(g_mlp.shape), _resident(w_up.shape),
                  _resident(w_dn.shape), _resident(g_fin.shape)],
        out_specs=row,
        out_shape=jax.ShapeDtypeStruct((n, D_MODEL), F32),
        compiler_params=_params("parallel"),
        name="post_final" if final else "post",
    )(x, o, w_o, g_mlp, w_up, w_dn, g_fin)


def _prep_weights(norm_mix, norm_mlp, norm_final, mla_w_in, mla_g_q, mla_w_qb, mla_g_kv,
                  mla_w_kvb, mla_w_o, sb_w_in, sb_w_o, mlp_w_up, mlp_w_down):
    half = MLA_ROPE // 2
    lat = MLA_Q_LORA + MLA_KV_LORA
    w_in_ext = jnp.concatenate(
        [mla_w_in, -mla_w_in[:, :, lat + half:], mla_w_in[:, :, lat:lat + half]], axis=2)
    n_mla = mla_w_qb.shape[0]
    qb = mla_w_qb.reshape(n_mla, MLA_Q_LORA, MLA_HEADS, MLA_NOPE + MLA_ROPE)
    qb_rot = jnp.concatenate([-qb[..., MLA_NOPE + half:], qb[..., MLA_NOPE:MLA_NOPE + half]], axis=-1)
    w_qb_ext = jnp.concatenate(
        [qb[..., :MLA_NOPE].reshape(n_mla, MLA_Q_LORA, MLA_QN),
         qb[..., MLA_NOPE:].reshape(n_mla, MLA_Q_LORA, MLA_QR),
         qb_rot.reshape(n_mla, MLA_Q_LORA, MLA_QR)], axis=2)
    kvb = mla_w_kvb.reshape(n_mla, MLA_KV_LORA, MLA_HEADS, MLA_NOPE + MLA_V)
    w_k = kvb[..., :MLA_NOPE].reshape(n_mla, MLA_KV_LORA, MLA_QN).astype(BF16)
    w_v = kvb[..., MLA_NOPE:].reshape(n_mla, MLA_KV_LORA, MLA_VW).astype(BF16)
    sb_w_in = sb_w_in.astype(BF16)
    return dict(
        norm_mix=norm_mix[:, None, :], norm_mlp=norm_mlp[:, None, :], norm_final=norm_final[None, :],
        w_in_ext=w_in_ext.astype(BF16), g_q=mla_g_q[:, None, :], g_kv=mla_g_kv[:, None, :],
        w_qb_ext=w_qb_ext.astype(BF16), w_k=w_k, w_v=w_v, w_vt=jnp.swapaxes(w_v, 1, 2),
        mla_w_o=mla_w_o.astype(BF16), sb_w_in=sb_w_in,
        sb_w_vt=jnp.swapaxes(sb_w_in[:, :, 2 * SB_WIDTH:], 1, 2), sb_w_o=sb_w_o.astype(BF16),
        w_up=mlp_w_up.astype(BF16), w_dn=mlp_w_down.astype(BF16))


def _rope_tables(pos, reps):
    inv = ROPE_THETA ** (-jnp.arange(0, MLA_ROPE, 2, dtype=F32) / MLA_ROPE)
    ang = pos.astype(F32)[:, None] * inv[None, :]
    tile = lambda t: jnp.tile(t, (reps, LANES // (MLA_ROPE // 2)))
    return tile(jnp.cos(ang)), tile(jnp.sin(ang))


def _cat_rows(parts, t_pad, dtype):
    parts = [p.astype(dtype) for p in parts]
    rows = sum(p.shape[1] for p in parts)
    if t_pad > rows:
        parts.append(jnp.zeros((parts[0].shape[0], t_pad - rows) + parts[0].shape[2:], dtype))
    return parts[0] if len(parts) == 1 else jnp.concatenate(parts, axis=1)


def _rope_key_variants(kr):
    zero = jnp.zeros_like(kr)
    return jnp.stack([jnp.concatenate([kr, zero], axis=2), jnp.concatenate([zero, kr], axis=2)], axis=1)


def _trunk(x, caches, w):
    b, s, _ = x.shape
    n = b * s
    prompt = caches is None
    past = 0 if prompt else caches[0].shape[2]
    if prompt:
        mla_t = _row_block(s, MLA_TILE)
        sb_t = _row_block(s // 2, SB_TILE)
    else:
        tk = LANES
        t_valid = past + s
        t_pad = -(-t_valid // tk) * tk
        tm = _row_block(n)
    pos = past + jnp.arange(s, dtype=jnp.int32)
    cos_t, sin_t = _rope_tables(pos, 1 if prompt else b)
    xf = x.reshape(n, D_MODEL)
    ckv_l, kr_l, k_l, v_l = [], [], [], []
    for i in range(DEPTH):
        j = i // 2
        if i % 2 == 0:
            ckv, krope, qn, qr = _mla_proj(xf, w["norm_mix"][i], w["w_in_ext"][j], w["g_q"][j],
                                           w["g_kv"][j], w["w_qb_ext"][j], cos_t, sin_t)
            ckv_l.append(ckv.reshape(b, s, MLA_KV_LORA))
            krope = krope.reshape(b, s, MLA_ROPE)
            kr_l.append(krope)
            qn = qn.reshape(b, s, MLA_QN)
            qr = qr.reshape(b, s, MLA_QR)
            if prompt:
                kn, vt = _mla_kv(ckv, w["w_k"][j], w["w_vt"][j], tm=mla_t, blocks_per_batch=s // mla_t)
                o = _mla_attn_t(qn, qr, kn.reshape(b, s, MLA_QN),
                                _rope_key_variants(krope.astype(BF16)), vt, tq=mla_t, tk=mla_t)
            else:
                ckv_all = _cat_rows([caches[0][j], ckv_l[-1]], t_pad, F32)
                kn, v = _mla_kv(ckv_all.reshape(b * t_pad, MLA_KV_LORA), w["w_k"][j], w["w_v"][j],
                                tm=_row_block(b * t_pad))
                kr2 = _rope_key_variants(_cat_rows([caches[1][j], krope], t_pad, BF16))
                o = _mla_attn(qn, qr, kn.reshape(b, t_pad, MLA_QN), kr2, v.reshape(b, t_pad, MLA_VW),
                              q_off=past, t_valid=t_valid)
            w_o = w["mla_w_o"][j]
        else:
            if prompt:
                q, k, v, kb, vb = _sb_proj(xf, w["norm_mix"][i], w["sb_w_in"][j], w["sb_w_vt"][j],
                                           tm=sb_t, blocks_per_batch=s // sb_t)
                o = _sb_attn_t(q.reshape(b, s, SB_WIDTH), kb.reshape(b, s, SB_WIDTH), vb, tq=sb_t)
            else:
                q, k, v, kb, vb = _sb_proj(xf, w["norm_mix"][i], w["sb_w_in"][j], w["sb_w_vt"][j], tm=tm)
                flat = lambda c: c[j].reshape(b, past, SB_WIDTH)
                k_all = _cat_rows([flat(caches[2]), kb.reshape(b, s, SB_WIDTH)], t_pad, BF16)
                v_all = _cat_rows([flat(caches[3]), vb.reshape(b, s, SB_WIDTH)], t_pad, BF16)
                o = _sb_attn(q.reshape(b, s, SB_WIDTH), k_all, v_all, q_off=past, tk=tk)
            k_l.append(k.reshape(b, s, SB_HEADS, SB_HEAD_DIM))
            v_l.append(v.reshape(b, s, SB_HEADS, SB_HEAD_DIM))
            w_o = w["sb_w_o"][j]
        xf = _post(xf, o.reshape(n, D_MODEL), w_o, w["norm_mlp"][i], w["w_up"][i], w["w_dn"][i],
                   w["norm_final"], final=(i == DEPTH - 1))
    return (xf.reshape(b, s, D_MODEL), jnp.stack(ckv_l), jnp.stack(kr_l), jnp.stack(k_l),
            jnp.stack(v_l))


def kernel(x_prompt, x_sample, cache_mla_ckv, cache_mla_krope, cache_sb_k, cache_sb_v, norm_mix, norm_mlp, norm_final, mla_w_in, mla_g_q, mla_w_qb, mla_g_kv, mla_w_kvb, mla_w_o, sb_w_in, sb_w_o, mlp_w_up, mlp_w_down):
    w = _prep_weights(norm_mix, norm_mlp, norm_final, mla_w_in, mla_g_q, mla_w_qb, mla_g_kv,
                      mla_w_kvb, mla_w_o, sb_w_in, sb_w_o, mlp_w_up, mlp_w_down)
    y_p, p_ckv, p_kr, p_k, p_v = _trunk(x_prompt, None, w)
    y_s, s_ckv, s_kr, s_k, s_v = _trunk(
        x_sample, (cache_mla_ckv, cache_mla_krope, cache_sb_k, cache_sb_v), w)
    return (y_p, y_s, p_ckv, p_kr, p_k, p_v, s_ckv, s_kr, s_k, s_v)
```

```python
import functools
import math

import jax
import jax.numpy as jnp
from jax import lax
from jax.experimental import pallas as pl
from jax.experimental.pallas import tpu as pltpu

D_MODEL = 1024
DEPTH = 4
CHUNK = 64
MLA_HEADS = 8
MLA_NOPE = 128
MLA_ROPE = 64
MLA_V = 128
MLA_Q_LORA = 512
MLA_KV_LORA = 256
ROPE_THETA = 10000.0
SB_HEADS = 16
SB_HEAD_DIM = 64
SB_WIDTH = SB_HEADS * SB_HEAD_DIM
D_FF = 4 * D_MODEL
EPS = 1e-6

LANES = 128
MLA_IN_EXT = MLA_Q_LORA + MLA_KV_LORA + 2 * MLA_ROPE
MLA_QN = MLA_HEADS * MLA_NOPE
MLA_QR = MLA_HEADS * MLA_ROPE
MLA_VW = MLA_HEADS * MLA_V
VMEM_LIMIT_BYTES = 56 * 1024 * 1024
MLA_TILE = 512
SB_TILE = 256
CAST_BLOCK_BYTES = 4 * 1024 * 1024
LOG2E = math.log2(math.e)
LN2 = math.log(2.0)

SB_LOG2_ZERO = 151.0

F32 = jnp.float32
BF16 = jnp.bfloat16
NT = (((1,), (1,)), ((), ()))


def _rms(x, g):
    return x * lax.rsqrt(jnp.mean(x * x, axis=-1, keepdims=True) + EPS) * g


def _params(*sem):
    return pltpu.CompilerParams(dimension_semantics=sem, vmem_limit_bytes=VMEM_LIMIT_BYTES)


def _resident(shape):
    nd = len(shape)
    return pl.BlockSpec(shape, lambda *_: (0,) * nd, pipeline_mode=pl.Buffered(1))


def _row_block(n, cap=512):
    for tm in (512, 256, 128, 64, 32, 16, 8):
        if tm <= cap and n % tm == 0:
            return tm
    raise ValueError(f"row count {n} is not a multiple of 8")


def _softplus2(z2):
    return jnp.maximum(z2, 0.0) + jnp.log(1.0 + jnp.exp2(-jnp.abs(z2))) * (1.0 / LN2)


def _mla_proj_kernel(x_ref, gmix_ref, win_ref, gq_ref, gkv_ref, wqb_ref, cos_ref, sin_ref,
                     ckv_ref, krope_ref, qn_ref, qr_ref):
    scale = (MLA_NOPE + MLA_ROPE) ** -0.5 * LOG2E
    h = _rms(x_ref[...], gmix_ref[...]).astype(BF16)
    a = jnp.dot(h, win_ref[...], preferred_element_type=F32)
    c_q = _rms(a[:, :MLA_Q_LORA], gq_ref[...])
    c_kv = _rms(a[:, MLA_Q_LORA:MLA_Q_LORA + MLA_KV_LORA], gkv_ref[...])
    ckv_ref[...] = c_kv
    cos = cos_ref[...]
    sin = sin_ref[...]
    kr_pair = a[:, MLA_Q_LORA + MLA_KV_LORA:]
    lane = lax.broadcasted_iota(jnp.int32, kr_pair.shape, 1)
    u = kr_pair * jnp.where(lane < MLA_ROPE, cos, sin)
    krope_ref[...] = (u + pltpu.roll(u, MLA_ROPE, axis=1))[:, :MLA_ROPE]
    q = jnp.dot(c_q.astype(BF16), wqb_ref[...], preferred_element_type=F32)
    qn_ref[...] = (q[:, :MLA_QN] * scale).astype(BF16)
    cos4 = jnp.concatenate([cos] * (MLA_QR // LANES), axis=1)
    sin4 = jnp.concatenate([sin] * (MLA_QR // LANES), axis=1)
    qr = q[:, MLA_QN:MLA_QN + MLA_QR] * cos4 + q[:, MLA_QN + MLA_QR:] * sin4
    qr_ref[...] = (qr * scale).astype(BF16)


def _mla_proj(x, g_mix, w_in_ext, g_q, g_kv, w_qb_ext, cos_t, sin_t):
    n = x.shape[0]
    tm = min(_row_block(n), _row_block(cos_t.shape[0]))
    n_tab = cos_t.shape[0] // tm
    row = lambda w: pl.BlockSpec((tm, w), lambda i: (i, 0))
    tab = pl.BlockSpec((tm, LANES), lambda i: (i % n_tab, 0))
    return pl.pallas_call(
        _mla_proj_kernel,
        grid=(n // tm,),
        in_specs=[row(D_MODEL), _resident(g_mix.shape), _resident(w_in_ext.shape),
                  _resident(g_q.shape), _resident(g_kv.shape), _resident(w_qb_ext.shape),
                  tab, tab],
        out_specs=[row(MLA_KV_LORA), row(MLA_ROPE), row(MLA_QN), row(MLA_QR)],
        out_shape=[jax.ShapeDtypeStruct((n, MLA_KV_LORA), F32),
                   jax.ShapeDtypeStruct((n, MLA_ROPE), F32),
                   jax.ShapeDtypeStruct((n, MLA_QN), BF16),
                   jax.ShapeDtypeStruct((n, MLA_QR), BF16)],
        compiler_params=_params("parallel"),
        name="mla_proj",
    )(x, g_mix, w_in_ext, g_q, g_kv, w_qb_ext, cos_t, sin_t)


def _mla_kv_kernel(ckv_ref, wk_ref, wv_ref, kn_ref, v_ref, *, transposed):
    c = ckv_ref[...].astype(BF16)
    kn_ref[...] = jnp.dot(c, wk_ref[...], preferred_element_type=F32).astype(BF16)
    if transposed:
        v_ref[0, 0] = lax.dot_general(wv_ref[...], c, NT, preferred_element_type=F32).astype(BF16)
    else:
        v_ref[...] = jnp.dot(c, wv_ref[...], preferred_element_type=F32).astype(BF16)


def _mla_kv(ckv, w_k, w_v, *, tm, blocks_per_batch=None):
    n = ckv.shape[0]
    transposed = blocks_per_batch is not None
    row = lambda w: pl.BlockSpec((tm, w), lambda i: (i, 0))
    if transposed:
        nkb = blocks_per_batch
        v_spec = pl.BlockSpec((1, 1, MLA_VW, tm), lambda i: (i // nkb, i % nkb, 0, 0))
        v_shape = jax.ShapeDtypeStruct((n // (tm * nkb), nkb, MLA_VW, tm), BF16)
    else:
        v_spec, v_shape = row(MLA_VW), jax.ShapeDtypeStruct((n, MLA_VW), BF16)
    return pl.pallas_call(
        functools.partial(_mla_kv_kernel, transposed=transposed),
        grid=(n // tm,),
        in_specs=[row(MLA_KV_LORA), _resident(w_k.shape), _resident(w_v.shape)],
        out_specs=[row(MLA_QN), v_spec],
        out_shape=[jax.ShapeDtypeStruct((n, MLA_QN), BF16), v_shape],
        compiler_params=_params("parallel"),
        name="mla_kv_t" if transposed else "mla_kv",
    )(ckv, w_k, w_v)


def _mla_attn_t_kernel(qn_ref, qr_ref, kn_ref, kr_ref, vt_ref, o_ref, acc_ref, *, tq, tk):
    qi = pl.program_id(2)
    p0 = qi * tq
    qr = qr_ref[0]
    qs = [jnp.concatenate([qn_ref[0, :, hh * LANES:(hh + 1) * LANES], qr], axis=1)
          for hh in range(2)]
    acc_ref[...] = jnp.zeros_like(acc_ref)

    def scores(kb, hh):
        start = pl.multiple_of(kb * tk, tk)
        k = jnp.concatenate([kn_ref[0, pl.ds(start, tk), hh * LANES:(hh + 1) * LANES],
                             kr_ref[0, hh, pl.ds(start, tk), :]], axis=1)
        return lax.dot_general(k, qs[hh], NT, preferred_element_type=F32)

    def consume(kb, hh, st, m, l):
        m_new = jnp.maximum(m, jnp.max(st, axis=0, keepdims=True))
        alpha = jnp.exp2(m - m_new)
        pt = jnp.exp2(st - m_new)
        l = alpha * l + jnp.sum(pt, axis=0, keepdims=True)
        pv = jnp.dot(vt_ref[0, kb, hh * MLA_V:(hh + 1) * MLA_V, :], pt.astype(BF16),
                     preferred_element_type=F32)
        acc_ref[hh] = alpha * acc_ref[hh] + pv
        return m_new, l

    def step(kb, carry):
        return tuple(consume(kb, hh, scores(kb, hh), *carry[hh]) for hh in range(2))

    def steps(n):
        def run(j, carry):
            base = n * j
            st = [scores(base, hh) for hh in range(2)]
            for i in range(n):
                nxt = [scores(base + i + 1, hh) for hh in range(2)] if i + 1 < n else None
                carry = tuple(consume(base + i, hh, st[hh], *carry[hh]) for hh in range(2))
                st = nxt
            return carry
        return run

    n_full = p0 // tk
    init = (jnp.full((1, tq), -1e30, F32), jnp.zeros((1, tq), F32))
    carry = lax.fori_loop(0, n_full // 4, steps(4), (init, init))
    carry = lax.fori_loop(2 * (n_full // 4), n_full // 2, steps(2), carry)
    carry = lax.fori_loop(2 * (n_full // 2), n_full, step, carry)
    k_pos = p0 + lax.broadcasted_iota(jnp.int32, (tk, tq), 0)
    q_pos = p0 + lax.broadcasted_iota(jnp.int32, (tk, tq), 1)
    visible = k_pos // CHUNK <= q_pos // CHUNK
    for hh in range(2):
        m, l = carry[hh]
        st = jnp.where(visible, scores(n_full, hh), -jnp.inf)
        _, l = consume(n_full, hh, st, m, l)
        o = acc_ref[hh] * (1.0 / l)
        o_ref[0, :, hh * MLA_V:(hh + 1) * MLA_V] = o.T.astype(o_ref.dtype)


def _mla_attn_t(qn, qr, kn, kr2, vt, *, tq, tk):
    b, s, _ = qn.shape
    nkb = vt.shape[1]
    assert tq == tk and s % tq == 0 and kn.shape[1] == s and nkb * tk == s
    kern = functools.partial(_mla_attn_t_kernel, tq=tq, tk=tk)
    return pl.pallas_call(
        kern,
        grid=(b, MLA_HEADS // 2, s // tq),
        in_specs=[pl.BlockSpec((1, tq, 2 * LANES), lambda bi, hp, qi: (bi, qi, hp)),
                  pl.BlockSpec((1, tq, LANES), lambda bi, hp, qi: (bi, qi, hp)),
                  pl.BlockSpec((1, s, 2 * LANES), lambda bi, hp, qi: (bi, 0, hp)),
                  pl.BlockSpec((1, 2, s, LANES), lambda bi, hp, qi: (bi, 0, 0, 0)),
                  pl.BlockSpec((1, nkb, 2 * MLA_V, tk), lambda bi, hp, qi: (bi, 0, hp, 0))],
        out_specs=pl.BlockSpec((1, tq, 2 * MLA_V), lambda bi, hp, qi: (bi, qi, hp)),
        out_shape=jax.ShapeDtypeStruct((b, s, MLA_VW), BF16),
        scratch_shapes=[pltpu.VMEM((2, MLA_V, tq), F32)],
        compiler_params=_params("parallel", "parallel", "arbitrary"),
        name="mla_attn_t",
    )(qn, qr, kn, kr2, vt)


def _mla_attn_kernel(qn_ref, qr_ref, kn_ref, kr_ref, v_ref, o_ref, *, q_off, t_valid):
    sq = qn_ref.shape[1]
    t = kn_ref.shape[1]
    q_pos = q_off + lax.broadcasted_iota(jnp.int32, (sq, t), 0)
    k_pos = lax.broadcasted_iota(jnp.int32, (sq, t), 1)
    visible = (k_pos // CHUNK <= q_pos // CHUNK) & (k_pos < t_valid)
    for h in range(MLA_HEADS):
        cols = slice(h * LANES, (h + 1) * LANES)
        pair = slice((h // 2) * LANES, (h // 2 + 1) * LANES)
        q = jnp.concatenate([qn_ref[0, :, cols], qr_ref[0, :, pair]], axis=1)
        k = jnp.concatenate([kn_ref[0, :, cols], kr_ref[0, h % 2]], axis=1)
        s = jnp.where(visible, lax.dot_general(q, k, NT, preferred_element_type=F32), -jnp.inf)
        p = jnp.exp2(s - jnp.max(s, axis=1, keepdims=True))
        o = jnp.dot(p.astype(BF16), v_ref[0, :, cols], preferred_element_type=F32)
        o_ref[0, :, cols] = (o / jnp.sum(p, axis=1, keepdims=True)).astype(o_ref.dtype)


def _stream_block(*shape):
    return pl.BlockSpec((1,) + shape, lambda bi: (bi,) + (0,) * len(shape))


def _mla_attn(qn, qr, kn, kr2, v, *, q_off, t_valid):
    b, sq, _ = qn.shape
    t = kn.shape[1]
    kern = functools.partial(_mla_attn_kernel, q_off=q_off, t_valid=t_valid)
    return pl.pallas_call(
        kern,
        grid=(b,),
        in_specs=[_stream_block(sq, MLA_QN), _stream_block(sq, MLA_QR), _stream_block(t, MLA_QN),
                  _stream_block(2, t, LANES), _stream_block(t, MLA_VW)],
        out_specs=_stream_block(sq, MLA_VW),
        out_shape=jax.ShapeDtypeStruct((b, sq, MLA_VW), BF16),
        compiler_params=_params("parallel"),
        name="mla_attn",
    )(qn, qr, kn, kr2, v)


def _sb_proj_kernel(x_ref, g_ref, w_ref, wvt_ref, q_ref, k_ref, v_ref, kb_ref, vb_ref, *, transposed):
    h = _rms(x_ref[...], g_ref[...]).astype(BF16)
    qkv = jnp.dot(h, w_ref[...], preferred_element_type=F32)
    q_ref[...] = (qkv[:, :SB_WIDTH] * (SB_HEAD_DIM ** -0.5 * LOG2E)).astype(BF16)
    k = qkv[:, SB_WIDTH:2 * SB_WIDTH]
    v = qkv[:, 2 * SB_WIDTH:]
    k_ref[...] = k
    v_ref[...] = v
    kb_ref[...] = k.astype(BF16)
    if transposed:
        vb_ref[0, 0] = lax.dot_general(wvt_ref[...], h, NT, preferred_element_type=F32).astype(BF16)
    else:
        vb_ref[...] = v.astype(BF16)


def _sb_proj(x, g, w_in, w_vt, *, tm, blocks_per_batch=None):
    n = x.shape[0]
    transposed = blocks_per_batch is not None
    row = pl.BlockSpec((tm, SB_WIDTH), lambda i: (i, 0))
    if transposed:
        nkb = blocks_per_batch
        vb_spec = pl.BlockSpec((1, 1, SB_WIDTH, tm), lambda i: (i // nkb, i % nkb, 0, 0))
        vb_shape = jax.ShapeDtypeStruct((n // (tm * nkb), nkb, SB_WIDTH, tm), BF16)
    else:
        vb_spec, vb_shape = row, jax.ShapeDtypeStruct((n, SB_WIDTH), BF16)
    return pl.pallas_call(
        functools.partial(_sb_proj_kernel, transposed=transposed),
        grid=(n // tm,),
        in_specs=[pl.BlockSpec((tm, D_MODEL), lambda i: (i, 0)), _resident(g.shape),
                  _resident(w_in.shape), _resident(w_vt.shape)],
        out_specs=[row, row, row, row, vb_spec],
        out_shape=[jax.ShapeDtypeStruct((n, SB_WIDTH), BF16),
                   jax.ShapeDtypeStruct((n, SB_WIDTH), F32),
                   jax.ShapeDtypeStruct((n, SB_WIDTH), F32),
                   jax.ShapeDtypeStruct((n, SB_WIDTH), BF16),
                   vb_shape],
        compiler_params=_params("parallel"),
        name="sb_proj_t" if transposed else "sb_proj",
    )(x, g, w_in, w_vt)


def _sb_attn_t_kernel(q_ref, k_ref, vt_ref, o_ref, *, tq, groups):
    tk = tq
    qi = pl.program_id(2)
    prev = jnp.maximum(qi - 1, 0)
    has_prev = (qi > 0).astype(F32)
    lane_head = lax.broadcasted_iota(jnp.int32, (tq, LANES), 1) // SB_HEAD_DIM
    group = lambda g: slice(g * LANES, (g + 1) * LANES)
    heads = [(g, hh) for g in range(groups) for hh in range(2)]
    q2 = [q_ref[0, :, group(g)] for g in range(groups)]
    qh = {h: jnp.where(lane_head == h[1], q2[h[0]], jnp.zeros_like(q2[h[0]])) for h in heads}
    key_row = lax.broadcasted_iota(jnp.int32, (tk, tk), 0)
    col = lax.broadcasted_iota(jnp.int32, (tk, tk), 1)
    later = (key_row < col).astype(BF16)
    causal = key_row < col

    def logits(kb, h):
        start = pl.multiple_of(kb * tk, tk)
        return lax.dot_general(k_ref[0, pl.ds(start, tk), group(h[0])], qh[h], NT,
                               preferred_element_type=F32)

    def suffix_in_block(sp):
        return jnp.dot(later, sp.astype(BF16), preferred_element_type=F32)

    def values(kb, h, a):
        return jnp.dot(vt_ref[0, kb, group(h[0]), :], a, preferred_element_type=F32)

    def col_sum(x):
        return jnp.sum(x, axis=0, keepdims=True)

    def window():
        z_diag = {h: logits(qi, h) for h in heads}
        z_prev = {h: logits(prev, h) for h in heads}
        sp_diag = {h: _softplus2(z_diag[h]) for h in heads}
        sp_prev = {h: _softplus2(z_prev[h]) for h in heads}
        spm = {h: jnp.where(causal, sp_diag[h], 0.0) for h in heads}
        suf_diag = {h: suffix_in_block(spm[h]) for h in heads}
        suf_prev = {h: suffix_in_block(sp_prev[h]) for h in heads}
        diag_sum = {h: col_sum(spm[h]) for h in heads}
        a_diag = {h: jnp.where(causal, jnp.exp2(z_diag[h] - sp_diag[h] - suf_diag[h]), 0.0).astype(BF16)
                  for h in heads}
        a_prev = {h: jnp.exp2(z_prev[h] - sp_prev[h] - suf_prev[h] - diag_sum[h]).astype(BF16)
                  for h in heads}
        pv_diag = {h: values(qi, h, a_diag[h]) for h in heads}
        pv_prev = {h: values(prev, h, a_prev[h]) for h in heads}
        carry = {h: diag_sum[h] + has_prev * col_sum(sp_prev[h]) for h in heads}
        return carry, {h: pv_diag[h] + has_prev * pv_prev[h] for h in heads}

    def earlier(kb, hs, carry, acc):
        zt = [logits(kb, h) for h in hs]
        sp = [_softplus2(z) for z in zt]
        suf = [suffix_in_block(s) for s in sp]
        a = [jnp.exp2(zt[i] - sp[i] - suf[i] - carry[i]).astype(BF16) for i in range(len(hs))]
        acc = [acc[i] + values(kb, hs[i], a[i]) for i in range(len(hs))]
        return [carry[i] + col_sum(sp[i]) for i in range(len(hs))], acc

    def unfinished(c):
        return jnp.minimum(jnp.min(c[0]), jnp.min(c[1])) < SB_LOG2_ZERO

    def cond(state):
        return jnp.logical_and(state[0] >= 0, state[1])

    carry, acc = window()
    row_head = lax.broadcasted_iota(jnp.int32, (LANES, tq), 0) // SB_HEAD_DIM
    for g in range(groups):
        hs = [(g, 0), (g, 1)]

        def body(state, hs=hs):
            kb, _, c, a = state
            c, a = earlier(kb, hs, c, a)
            return kb - 1, unfinished(c), c, a

        c0 = [carry[h] for h in hs]
        a0, a1 = lax.while_loop(cond, body, (prev - 1, unfinished(c0), c0, [acc[h] for h in hs]))[3]
        o_ref[0, :, group(g)] = jnp.where(row_head == 0, a0, a1).T.astype(o_ref.dtype)


def _sb_attn_t(q, k, vt, *, tq, groups=2):
    b, s, _ = q.shape
    nkb = vt.shape[1]
    assert s % tq == 0 and nkb * tq == s and vt.shape[3] == tq
    pairs = SB_WIDTH // LANES
    gw = groups * LANES
    return pl.pallas_call(
        functools.partial(_sb_attn_t_kernel, tq=tq, groups=groups),
        grid=(b, pairs // groups, s // tq),
        in_specs=[pl.BlockSpec((1, tq, gw), lambda bi, h, qi: (bi, qi, h)),
                  pl.BlockSpec((1, s, gw), lambda bi, h, qi: (bi, 0, h)),
                  pl.BlockSpec((1, nkb, gw, tq), lambda bi, h, qi: (bi, 0, h, 0))],
        out_specs=pl.BlockSpec((1, tq, gw), lambda bi, h, qi: (bi, qi, h)),
        out_shape=jax.ShapeDtypeStruct((b, s, SB_WIDTH), BF16),
        compiler_params=_params("parallel", "parallel", "arbitrary"),
        name="sb_attn_t",
    )(q, k, vt)


def _sb_attn_kernel(q_ref, k_ref, v_ref, o_ref, *, tk, q_off):
    sq = q_ref.shape[1]
    pairs = SB_WIDTH // LANES
    rows = 2 * pairs * sq
    first = q_off // tk
    lane_head = lax.broadcasted_iota(jnp.int32, (sq, LANES), 1) // SB_HEAD_DIM
    group = lambda p: slice(p * LANES, (p + 1) * LANES)
    lhs = []
    for p in range(pairs):
        q2 = q_ref[0, :, group(p)]
        lhs.append(jnp.concatenate([jnp.where(lane_head == hh, q2, jnp.zeros_like(q2))
                                    for hh in range(2)], axis=0))
    later = (lax.broadcasted_iota(jnp.int32, (tk, tk), 0)
             > lax.broadcasted_iota(jnp.int32, (tk, tk), 1)).astype(BF16)

    def block(kb, carry, acc, masked):
        start = pl.multiple_of(kb * tk, tk)
        z = jnp.concatenate(
            [lax.dot_general(lhs[p], k_ref[0, pl.ds(start, tk), group(p)], NT,
                             preferred_element_type=F32) for p in range(pairs)], axis=0)
        sp = _softplus2(z)
        spm = sp
        if masked:
            q_pos = q_off + lax.rem(lax.broadcasted_iota(jnp.int32, (rows, tk), 0), sq)
            k_pos = start + lax.broadcasted_iota(jnp.int32, (rows, tk), 1)
            ok = k_pos < q_pos
            spm = jnp.where(ok, sp, 0.0)
        suffix = jnp.dot(spm.astype(BF16), later, preferred_element_type=F32) + carry
        a = jnp.exp2(z - sp - suffix)
        if masked:
            a = jnp.where(ok, a, 0.0)
        a = a.astype(BF16)
        pv = jnp.concatenate(
            [jnp.dot(a[2 * sq * p:2 * sq * (p + 1)], v_ref[0, pl.ds(start, tk), group(p)],
                     preferred_element_type=F32) for p in range(pairs)], axis=0)
        return carry + jnp.sum(spm, axis=1, keepdims=True), acc + pv

    carry, acc = block(first, jnp.zeros((rows, 1), F32), jnp.zeros((rows, LANES), F32), True)

    def cond(state):
        return jnp.logical_and(state[0] >= 0, state[1])

    def body(state):
        kb, _, carry, acc = state
        carry, acc = block(kb, carry, acc, False)
        return kb - 1, jnp.min(carry) < SB_LOG2_ZERO, carry, acc

    acc = lax.while_loop(cond, body, (first - 1, jnp.min(carry) < SB_LOG2_ZERO, carry, acc))[3]
    for p in range(pairs):
        both = acc[2 * sq * p:2 * sq * (p + 1)]
        o_ref[0, :, group(p)] = jnp.where(lane_head == 0, both[:sq], both[sq:]).astype(o_ref.dtype)


def _sb_attn(q, k, v, *, q_off, tk):
    b, sq, _ = q.shape
    t = k.shape[1]
    assert t % tk == 0 and q_off % tk == 0 and sq <= tk and q_off + tk <= t
    kern = functools.partial(_sb_attn_kernel, tk=tk, q_off=q_off)
    return pl.pallas_call(
        kern,
        grid=(b,),
        in_specs=[_stream_block(sq, SB_WIDTH), _stream_block(t, SB_WIDTH), _stream_block(t, SB_WIDTH)],
        out_specs=_stream_block(sq, SB_WIDTH),
        out_shape=jax.ShapeDtypeStruct((b, sq, SB_WIDTH), BF16),
        compiler_params=_params("parallel"),
        name="sb_attn",
    )(q, k, v)


def _post_kernel(x_ref, o_ref, wo_ref, g_ref, wup_ref, wdn_ref, gfin_ref, y_ref, *, final, ff_blk):
    x1 = x_ref[...] + jnp.dot(o_ref[...], wo_ref[...], preferred_element_type=F32)
    h = _rms(x1, g_ref[...]).astype(BF16)
    acc = x1
    for c in range(0, D_FF, ff_blk):
        up = jnp.dot(h, wup_ref[:, c:c + ff_blk], preferred_element_type=F32)
        act = jnp.square(jnp.maximum(up, 0.0)).astype(BF16)
        acc = acc + jnp.dot(act, wdn_ref[c:c + ff_blk, :], preferred_element_type=F32)
    y_ref[...] = _rms(acc, gfin_ref[...]) if final else acc


def _post(x, o, w_o, g_mlp, w_up, w_dn, g_fin, *, final):
    n = x.shape[0]
    tm = _row_block(n)
    row = pl.BlockSpec((tm, D_MODEL), lambda i: (i, 0))
    kern = functools.partial(_post_kernel, final=final, ff_blk=1024)
    return pl.pallas_call(
        kern,
        grid=(n // tm,),
        in_specs=[row, row, _resident(w_o.shape), _resident(g_mlp.shape), _resident(w_up.shape),
                  _resident(w_dn.shape), _resident(g_fin.shape)],
        out_specs=row,
        out_shape=jax.ShapeDtypeStruct((n, D_MODEL), F32),
        compiler_params=_params("parallel"),
        name="post_final" if final else "post",
    )(x, o, w_o, g_mlp, w_up, w_dn, g_fin)


def _cast_kernel(x_ref, o_ref):
    o_ref[...] = x_ref[...].astype(o_ref.dtype)


def _cast_bf16(w):
    cols = w.shape[-1]
    rows = w.size // cols
    tm = rows
    while tm * cols * 4 > CAST_BLOCK_BYTES and tm % 32 == 0:
        tm //= 2
    spec = pl.BlockSpec((tm, cols), lambda i: (i, 0))
    out = pl.pallas_call(
        _cast_kernel, grid=(rows // tm,), in_specs=[spec], out_specs=spec,
        out_shape=jax.ShapeDtypeStruct((rows, cols), BF16),
        compiler_params=_params("parallel"), name="cast_bf16",
    )(w.reshape(rows, cols))
    return out.reshape(w.shape)


def _prep_weights(norm_mix, norm_mlp, norm_final, mla_w_in, mla_g_q, mla_w_qb, mla_g_kv,
                  mla_w_kvb, mla_w_o, sb_w_in, sb_w_o, mlp_w_up, mlp_w_down):
    half = MLA_ROPE // 2
    lat = MLA_Q_LORA + MLA_KV_LORA
    w_in_ext = jnp.concatenate(
        [mla_w_in, -mla_w_in[:, :, lat + half:], mla_w_in[:, :, lat:lat + half]], axis=2)
    n_mla = mla_w_qb.shape[0]
    qb = mla_w_qb.reshape(n_mla, MLA_Q_LORA, MLA_HEADS, MLA_NOPE + MLA_ROPE)
    qb_rot = jnp.concatenate([-qb[..., MLA_NOPE + half:], qb[..., MLA_NOPE:MLA_NOPE + half]], axis=-1)
    w_qb_ext = jnp.concatenate(
        [qb[..., :MLA_NOPE].reshape(n_mla, MLA_Q_LORA, MLA_QN),
         qb[..., MLA_NOPE:].reshape(n_mla, MLA_Q_LORA, MLA_QR),
         qb_rot.reshape(n_mla, MLA_Q_LORA, MLA_QR)], axis=2)
    kvb = mla_w_kvb.reshape(n_mla, MLA_KV_LORA, MLA_HEADS, MLA_NOPE + MLA_V)
    w_k = kvb[..., :MLA_NOPE].reshape(n_mla, MLA_KV_LORA, MLA_QN).astype(BF16)
    w_v = kvb[..., MLA_NOPE:].reshape(n_mla, MLA_KV_LORA, MLA_VW).astype(BF16)
    sb_w_in = _cast_bf16(sb_w_in)
    return dict(
        norm_mix=norm_mix[:, None, :], norm_mlp=norm_mlp[:, None, :], norm_final=norm_final[None, :],
        w_in_ext=w_in_ext.astype(BF16), g_q=mla_g_q[:, None, :], g_kv=mla_g_kv[:, None, :],
        w_qb_ext=w_qb_ext.astype(BF16), w_k=w_k, w_v=w_v, w_vt=jnp.swapaxes(w_v, 1, 2),
        mla_w_o=_cast_bf16(mla_w_o), sb_w_in=sb_w_in,
        sb_w_vt=jnp.swapaxes(sb_w_in[:, :, 2 * SB_WIDTH:], 1, 2), sb_w_o=_cast_bf16(sb_w_o),
        w_up=_cast_bf16(mlp_w_up), w_dn=_cast_bf16(mlp_w_down))


def _rope_tables(pos, reps):
    inv = ROPE_THETA ** (-jnp.arange(0, MLA_ROPE, 2, dtype=F32) / MLA_ROPE)
    ang = pos.astype(F32)[:, None] * inv[None, :]
    tile = lambda t: jnp.tile(t, (reps, LANES // (MLA_ROPE // 2)))
    return tile(jnp.cos(ang)), tile(jnp.sin(ang))


def _cat_rows(parts, t_pad, dtype):
    parts = [p.astype(dtype) for p in parts]
    rows = sum(p.shape[1] for p in parts)
    if t_pad > rows:
        parts.append(jnp.zeros((parts[0].shape[0], t_pad - rows) + parts[0].shape[2:], dtype))
    return parts[0] if len(parts) == 1 else jnp.concatenate(parts, axis=1)


def _rope_key_variants(kr):
    zero = jnp.zeros_like(kr)
    return jnp.stack([jnp.concatenate([kr, zero], axis=2), jnp.concatenate([zero, kr], axis=2)], axis=1)


def _trunk(x, caches, w):
    b, s, _ = x.shape
    n = b * s
    prompt = caches is None
    past = 0 if prompt else caches[0].shape[2]
    if prompt:
        mla_t = _row_block(s, MLA_TILE)
        sb_t = _row_block(s // 2, SB_TILE)
    else:
        tk = LANES
        t_valid = past + s
        t_pad = -(-t_valid // tk) * tk
        tm = _row_block(n)
    pos = past + jnp.arange(s, dtype=jnp.int32)
    cos_t, sin_t = _rope_tables(pos, 1 if prompt else b)
    xf = x.reshape(n, D_MODEL)
    ckv_l, kr_l, k_l, v_l = [], [], [], []
    for i in range(DEPTH):
        j = i // 2
        if i % 2 == 0:
            ckv, krope, qn, qr = _mla_proj(xf, w["norm_mix"][i], w["w_in_ext"][j], w["g_q"][j],
                                           w["g_kv"][j], w["w_qb_ext"][j], cos_t, sin_t)
            ckv_l.append(ckv.reshape(b, s, MLA_KV_LORA))
            krope = krope.reshape(b, s, MLA_ROPE)
            kr_l.append(krope)
            qn = qn.reshape(b, s, MLA_QN)
            qr = qr.reshape(b, s, MLA_QR)
            if prompt:
                kn, vt = _mla_kv(ckv, w["w_k"][j], w["w_vt"][j], tm=mla_t, blocks_per_batch=s // mla_t)
                o = _mla_attn_t(qn, qr, kn.reshape(b, s, MLA_QN),
                                _rope_key_variants(krope.astype(BF16)), vt, tq=mla_t, tk=mla_t)
            else:
                ckv_all = _cat_rows([caches[0][j], ckv_l[-1]], t_pad, F32)
                kn, v = _mla_kv(ckv_all.reshape(b * t_pad, MLA_KV_LORA), w["w_k"][j], w["w_v"][j],
                                tm=_row_block(b * t_pad))
                kr2 = _rope_key_variants(_cat_rows([caches[1][j], krope], t_pad, BF16))
                o = _mla_attn(qn, qr, kn.reshape(b, t_pad, MLA_QN), kr2, v.reshape(b, t_pad, MLA_VW),
                              q_off=past, t_valid=t_valid)
            w_o = w["mla_w_o"][j]
        else:
            if prompt:
                q, k, v, kb, vb = _sb_proj(xf, w["norm_mix"][i], w["sb_w_in"][j], w["sb_w_vt"][j],
                                           tm=sb_t, blocks_per_batch=s // sb_t)
                o = _sb_attn_t(q.reshape(b, s, SB_WIDTH), kb.reshape(b, s, SB_WIDTH), vb, tq=sb_t)
            else:
                q, k, v, kb, vb = _sb_proj(xf, w["norm_mix"][i], w["sb_w_in"][j], w["sb_w_vt"][j], tm=tm)
                flat = lambda c: c[j].reshape(b, past, SB_WIDTH)
                k_all = _cat_rows([flat(caches[2]), kb.reshape(b, s, SB_WIDTH)], t_pad, BF16)
                v_all = _cat_rows([flat(caches[3]), vb.reshape(b, s, SB_WIDTH)], t_pad, BF16)
                o = _sb_attn(q.reshape(b, s, SB_WIDTH), k_all, v_all, q_off=past, tk=tk)
            k_l.append(k.reshape(b, s, SB_HEADS, SB_HEAD_DIM))
            v_l.append(v.reshape(b, s, SB_HEADS, SB_HEAD_DIM))
            w_o = w["sb_w_o"][j]
        xf = _post(xf, o.reshape(n, D_MODEL), w_o, w["norm_mlp"][i], w["w_up"][i], w["w_dn"][i],
                   w["norm_final"], final=(i == DEPTH - 1))
    return (xf.reshape(b, s, D_MODEL), jnp.stack(ckv_l), jnp.stack(kr_l), jnp.stack(k_l),
            jnp.stack(v_l))


def kernel(x_prompt, x_sample, cache_mla_ckv, cache_mla_krope, cache_sb_k, cache_sb_v, norm_mix, norm_mlp, norm_final, mla_w_in, mla_g_q, mla_w_qb, mla_g_kv, mla_w_kvb, mla_w_o, sb_w_in, sb_w_o, mlp_w_up, mlp_w_down):
    w = _prep_weights(norm_mix, norm_mlp, norm_final, mla_w_in, mla_g_q, mla_w_qb, mla_g_kv,
                      mla_w_kvb, mla_w_o, sb_w_in, sb_w_o, mlp_w_up, mlp_w_down)
    y_p, p_ckv, p_kr, p_k, p_v = _trunk(x_prompt, None, w)
    y_s, s_ckv, s_kr, s_k, s_v = _trunk(
        x_sample, (cache_mla_ckv, cache_mla_krope, cache_sb_k, cache_sb_v), w)
    return (y_p, y_s, p_ckv, p_kr, p_k, p_v, s_ckv, s_kr, s_k, s_v)
```

```python
import functools
import math

import jax
import jax.numpy as jnp
from jax import lax
from jax.experimental import pallas as pl
from jax.experimental.pallas import tpu as pltpu

D_MODEL = 1024
DEPTH = 4
CHUNK = 64
MLA_HEADS = 8
MLA_NOPE = 128
MLA_ROPE = 64
MLA_V = 128
MLA_Q_LORA = 512
MLA_KV_LORA = 256
ROPE_THETA = 10000.0
SB_HEADS = 16
SB_HEAD_DIM = 64
SB_WIDTH = SB_HEADS * SB_HEAD_DIM
D_FF = 4 * D_MODEL
EPS = 1e-6

LANES = 128
MLA_IN_EXT = MLA_Q_LORA + MLA_KV_LORA + 2 * MLA_ROPE
MLA_QN = MLA_HEADS * MLA_NOPE
MLA_QR = MLA_HEADS * MLA_ROPE
MLA_VW = MLA_HEADS * MLA_V
VMEM_LIMIT_BYTES = 56 * 1024 * 1024
MLA_TILE = 512
SB_TILE = 256
CAST_BLOCK_BYTES = 4 * 1024 * 1024
LOG2E = math.log2(math.e)
LN2 = math.log(2.0)

SB_LOG2_ZERO = 151.0

F32 = jnp.float32
BF16 = jnp.bfloat16
NT = (((1,), (1,)), ((), ()))


def _rms(x, g):
    return x * lax.rsqrt(jnp.mean(x * x, axis=-1, keepdims=True) + EPS) * g


def _params(*sem):
    return pltpu.CompilerParams(dimension_semantics=sem, vmem_limit_bytes=VMEM_LIMIT_BYTES)


def _resident(shape):
    nd = len(shape)
    return pl.BlockSpec(shape, lambda *_: (0,) * nd, pipeline_mode=pl.Buffered(1))


def _row_block(n, cap=512):
    for tm in (512, 256, 128, 64, 32, 16, 8):
        if tm <= cap and n % tm == 0:
            return tm
    raise ValueError(f"row count {n} is not a multiple of 8")


def _softplus2(z2):
    return jnp.maximum(z2, 0.0) + jnp.log(1.0 + jnp.exp2(-jnp.abs(z2))) * (1.0 / LN2)


def _mla_proj_kernel(x_ref, gmix_ref, win_ref, gq_ref, gkv_ref, wqb_ref, cos_ref, sin_ref,
                     ckv_ref, krope_ref, qn_ref, qr_ref):
    scale = (MLA_NOPE + MLA_ROPE) ** -0.5 * LOG2E
    h = _rms(x_ref[...], gmix_ref[...]).astype(BF16)
    a = jnp.dot(h, win_ref[...], preferred_element_type=F32)
    c_q = _rms(a[:, :MLA_Q_LORA], gq_ref[...])
    c_kv = _rms(a[:, MLA_Q_LORA:MLA_Q_LORA + MLA_KV_LORA], gkv_ref[...])
    ckv_ref[...] = c_kv
    cos = cos_ref[...]
    sin = sin_ref[...]
    kr_pair = a[:, MLA_Q_LORA + MLA_KV_LORA:]
    lane = lax.broadcasted_iota(jnp.int32, kr_pair.shape, 1)
    u = kr_pair * jnp.where(lane < MLA_ROPE, cos, sin)
    krope_ref[...] = (u + pltpu.roll(u, MLA_ROPE, axis=1))[:, :MLA_ROPE]
    q = jnp.dot(c_q.astype(BF16), wqb_ref[...], preferred_element_type=F32)
    qn_ref[...] = (q[:, :MLA_QN] * scale).astype(BF16)
    cos4 = jnp.concatenate([cos] * (MLA_QR // LANES), axis=1)
    sin4 = jnp.concatenate([sin] * (MLA_QR // LANES), axis=1)
    qr = q[:, MLA_QN:MLA_QN + MLA_QR] * cos4 + q[:, MLA_QN + MLA_QR:] * sin4
    qr_ref[...] = (qr * scale).astype(BF16)


def _mla_proj(x, g_mix, w_in_ext, g_q, g_kv, w_qb_ext, cos_t, sin_t):
    n = x.shape[0]
    tm = min(_row_block(n), _row_block(cos_t.shape[0]))
    n_tab = cos_t.shape[0] // tm
    row = lambda w: pl.BlockSpec((tm, w), lambda i: (i, 0))
    tab = pl.BlockSpec((tm, LANES), lambda i: (i % n_tab, 0))
    return pl.pallas_call(
        _mla_proj_kernel,
        grid=(n // tm,),
        in_specs=[row(D_MODEL), _resident(g_mix.shape), _resident(w_in_ext.shape),
                  _resident(g_q.shape), _resident(g_kv.shape), _resident(w_qb_ext.shape),
                  tab, tab],
        out_specs=[row(MLA_KV_LORA), row(MLA_ROPE), row(MLA_QN), row(MLA_QR)],
        out_shape=[jax.ShapeDtypeStruct((n, MLA_KV_LORA), F32),
                   jax.ShapeDtypeStruct((n, MLA_ROPE), F32),
                   jax.ShapeDtypeStruct((n, MLA_QN), BF16),
                   jax.ShapeDtypeStruct((n, MLA_QR), BF16)],
        compiler_params=_params("parallel"),
        name="mla_proj",
    )(x, g_mix, w_in_ext, g_q, g_kv, w_qb_ext, cos_t, sin_t)


def _mla_kv_kernel(ckv_ref, wk_ref, wvt_ref, kn_ref, vt_ref):
    c = ckv_ref[...].astype(BF16)
    kn_ref[...] = jnp.dot(c, wk_ref[...], preferred_element_type=F32).astype(BF16)
    vt_ref[0, 0] = lax.dot_general(wvt_ref[...], c, NT, preferred_element_type=F32).astype(BF16)


def _mla_kv(ckv, w_k, w_vt, *, tm, blocks_per_batch):
    n = ckv.shape[0]
    nkb = blocks_per_batch
    row = lambda w: pl.BlockSpec((tm, w), lambda i: (i, 0))
    return pl.pallas_call(
        _mla_kv_kernel,
        grid=(n // tm,),
        in_specs=[row(MLA_KV_LORA), _resident(w_k.shape), _resident(w_vt.shape)],
        out_specs=[row(MLA_QN), pl.BlockSpec((1, 1, MLA_VW, tm), lambda i: (i // nkb, i % nkb, 0, 0))],
        out_shape=[jax.ShapeDtypeStruct((n, MLA_QN), BF16),
                   jax.ShapeDtypeStruct((n // (tm * nkb), nkb, MLA_VW, tm), BF16)],
        compiler_params=_params("parallel"),
        name="mla_kv_t",
    )(ckv, w_k, w_vt)


def _mla_attn_t_kernel(qn_ref, qr_ref, kn_ref, kr_ref, vt_ref, o_ref, acc_ref, *, tq, tk):
    qi = pl.program_id(2)
    p0 = qi * tq
    qr = qr_ref[0]
    qs = [jnp.concatenate([qn_ref[0, :, hh * LANES:(hh + 1) * LANES], qr], axis=1)
          for hh in range(2)]
    acc_ref[...] = jnp.zeros_like(acc_ref)

    def scores(kb, hh):
        start = pl.multiple_of(kb * tk, tk)
        k = jnp.concatenate([kn_ref[0, pl.ds(start, tk), hh * LANES:(hh + 1) * LANES],
                             kr_ref[0, hh, pl.ds(start, tk), :]], axis=1)
        return lax.dot_general(k, qs[hh], NT, preferred_element_type=F32)

    def consume(kb, hh, st, m, l):
        m_new = jnp.maximum(m, jnp.max(st, axis=0, keepdims=True))
        alpha = jnp.exp2(m - m_new)
        pt = jnp.exp2(st - m_new)
        l = alpha * l + jnp.sum(pt, axis=0, keepdims=True)
        pv = jnp.dot(vt_ref[0, kb, hh * MLA_V:(hh + 1) * MLA_V, :], pt.astype(BF16),
                     preferred_element_type=F32)
        acc_ref[hh] = alpha * acc_ref[hh] + pv
        return m_new, l

    def step(kb, carry):
        return tuple(consume(kb, hh, scores(kb, hh), *carry[hh]) for hh in range(2))

    def steps(n):
        def run(j, carry):
            base = n * j
            st = [scores(base, hh) for hh in range(2)]
            for i in range(n):
                nxt = [scores(base + i + 1, hh) for hh in range(2)] if i + 1 < n else None
                carry = tuple(consume(base + i, hh, st[hh], *carry[hh]) for hh in range(2))
                st = nxt
            return carry
        return run

    n_full = p0 // tk
    init = (jnp.full((1, tq), -1e30, F32), jnp.zeros((1, tq), F32))
    carry = lax.fori_loop(0, n_full // 4, steps(4), (init, init))
    carry = lax.fori_loop(2 * (n_full // 4), n_full // 2, steps(2), carry)
    carry = lax.fori_loop(2 * (n_full // 2), n_full, step, carry)
    k_pos = p0 + lax.broadcasted_iota(jnp.int32, (tk, tq), 0)
    q_pos = p0 + lax.broadcasted_iota(jnp.int32, (tk, tq), 1)
    visible = k_pos // CHUNK <= q_pos // CHUNK
    for hh in range(2):
        m, l = carry[hh]
        st = jnp.where(visible, scores(n_full, hh), -jnp.inf)
        _, l = consume(n_full, hh, st, m, l)
        o = acc_ref[hh] * (1.0 / l)
        o_ref[0, :, hh * MLA_V:(hh + 1) * MLA_V] = o.T.astype(o_ref.dtype)


def _mla_attn_t(qn, qr, kn, kr2, vt, *, tq, tk):
    b, s, _ = qn.shape
    nkb = vt.shape[1]
    assert tq == tk and s % tq == 0 and kn.shape[1] == s and nkb * tk == s
    kern = functools.partial(_mla_attn_t_kernel, tq=tq, tk=tk)
    return pl.pallas_call(
        kern,
        grid=(b, MLA_HEADS // 2, s // tq),
        in_specs=[pl.BlockSpec((1, tq, 2 * LANES), lambda bi, hp, qi: (bi, qi, hp)),
                  pl.BlockSpec((1, tq, LANES), lambda bi, hp, qi: (bi, qi, hp)),
                  pl.BlockSpec((1, s, 2 * LANES), lambda bi, hp, qi: (bi, 0, hp)),
                  pl.BlockSpec((1, 2, s, LANES), lambda bi, hp, qi: (bi, 0, 0, 0)),
                  pl.BlockSpec((1, nkb, 2 * MLA_V, tk), lambda bi, hp, qi: (bi, 0, hp, 0))],
        out_specs=pl.BlockSpec((1, tq, 2 * MLA_V), lambda bi, hp, qi: (bi, qi, hp)),
        out_shape=jax.ShapeDtypeStruct((b, s, MLA_VW), BF16),
        scratch_shapes=[pltpu.VMEM((2, MLA_V, tq), F32)],
        compiler_params=_params("parallel", "parallel", "arbitrary"),
        name="mla_attn_t",
    )(qn, qr, kn, kr2, vt)


def _mla_attn_kernel(qn_ref, qr_ref, cn_ref, rn_ref, cc_ref, rc_ref, wk_ref, wv_ref, o_ref, *, t_pad):
    sq = qn_ref.shape[1]
    past = cc_ref.shape[2]
    t_valid = past + sq

    def all_rows(cache, new):
        width = new.shape[1]
        return jnp.concatenate([cache.astype(BF16), new.astype(BF16),
                                jnp.zeros((t_pad - t_valid, width), BF16)], axis=0)

    latent = all_rows(cc_ref[0, 0], cn_ref[0])
    rope = all_rows(rc_ref[0, 0], rn_ref[0])
    r_in = lax.broadcasted_iota(jnp.int32, (MLA_ROPE, LANES), 0)
    r_out = lax.broadcasted_iota(jnp.int32, (MLA_ROPE, LANES), 1)
    rope_half = [jnp.dot(rope, (r_out == r_in + hh * MLA_ROPE).astype(BF16),
                         preferred_element_type=F32).astype(BF16) for hh in range(2)]
    q_pos = past + lax.broadcasted_iota(jnp.int32, (sq, t_pad), 0)
    k_pos = lax.broadcasted_iota(jnp.int32, (sq, t_pad), 1)
    visible = (k_pos // CHUNK <= q_pos // CHUNK) & (k_pos < t_valid)
    for pair in range(MLA_HEADS // 2):
        two = slice(pair * 2 * LANES, (pair + 1) * 2 * LANES)
        kn = jnp.dot(latent, wk_ref[:, two], preferred_element_type=F32).astype(BF16)
        v = jnp.dot(latent, wv_ref[:, two], preferred_element_type=F32).astype(BF16)
        qr = qr_ref[0, :, pair * LANES:(pair + 1) * LANES]
        for hh in range(2):
            cols = slice((2 * pair + hh) * LANES, (2 * pair + hh + 1) * LANES)
            half = slice(hh * LANES, (hh + 1) * LANES)
            q = jnp.concatenate([qn_ref[0, :, cols], qr], axis=1)
            k = jnp.concatenate([kn[:, half], rope_half[hh]], axis=1)
            s = jnp.where(visible, lax.dot_general(q, k, NT, preferred_element_type=F32), -jnp.inf)
            p = jnp.exp2(s - jnp.max(s, axis=1, keepdims=True))
            o = jnp.dot(p.astype(BF16), v[:, half], preferred_element_type=F32)
            o_ref[0, :, cols] = (o / jnp.sum(p, axis=1, keepdims=True)).astype(o_ref.dtype)


def _stream_block(*shape):
    return pl.BlockSpec((1,) + shape, lambda bi: (bi,) + (0,) * len(shape))


def _mla_attn(qn, qr, ckv_new, kr_new, ckv_cache, kr_cache, layer, w_k, w_v):
    b, sq, _ = qn.shape
    past = ckv_cache.shape[2]
    t_pad = -(-(past + sq) // LANES) * LANES
    cache = lambda width: pl.BlockSpec((1, 1, past, width), lambda bi: (layer, bi, 0, 0))
    return pl.pallas_call(
        functools.partial(_mla_attn_kernel, t_pad=t_pad),
        grid=(b,),
        in_specs=[_stream_block(sq, MLA_QN), _stream_block(sq, MLA_QR),
                  _stream_block(sq, MLA_KV_LORA), _stream_block(sq, MLA_ROPE),
                  cache(MLA_KV_LORA), cache(MLA_ROPE), _resident(w_k.shape), _resident(w_v.shape)],
        out_specs=_stream_block(sq, MLA_VW),
        out_shape=jax.ShapeDtypeStruct((b, sq, MLA_VW), BF16),
        compiler_params=_params("parallel"),
        name="mla_attn",
    )(qn, qr, ckv_new, kr_new, ckv_cache, kr_cache, w_k, w_v)


def _sb_proj_kernel(x_ref, g_ref, w_ref, wvt_ref, q_ref, k_ref, v_ref, kb_ref, vb_ref, *, transposed):
    h = _rms(x_ref[...], g_ref[...]).astype(BF16)
    qkv = jnp.dot(h, w_ref[...], preferred_element_type=F32)
    q_ref[...] = (qkv[:, :SB_WIDTH] * (SB_HEAD_DIM ** -0.5 * LOG2E)).astype(BF16)
    k = qkv[:, SB_WIDTH:2 * SB_WIDTH]
    v = qkv[:, 2 * SB_WIDTH:]
    k_ref[...] = k
    v_ref[...] = v
    kb_ref[...] = k.astype(BF16)
    if transposed:
        vb_ref[0, 0] = lax.dot_general(wvt_ref[...], h, NT, preferred_element_type=F32).astype(BF16)
    else:
        vb_ref[...] = v.astype(BF16)


def _sb_proj(x, g, w_in, w_vt, *, tm, blocks_per_batch=None):
    n = x.shape[0]
    transposed = blocks_per_batch is not None
    row = pl.BlockSpec((tm, SB_WIDTH), lambda i: (i, 0))
    if transposed:
        nkb = blocks_per_batch
        vb_spec = pl.BlockSpec((1, 1, SB_WIDTH, tm), lambda i: (i // nkb, i % nkb, 0, 0))
        vb_shape = jax.ShapeDtypeStruct((n // (tm * nkb), nkb, SB_WIDTH, tm), BF16)
    else:
        vb_spec, vb_shape = row, jax.ShapeDtypeStruct((n, SB_WIDTH), BF16)
    return pl.pallas_call(
        functools.partial(_sb_proj_kernel, transposed=transposed),
        grid=(n // tm,),
        in_specs=[pl.BlockSpec((tm, D_MODEL), lambda i: (i, 0)), _resident(g.shape),
                  _resident(w_in.shape), _resident(w_vt.shape)],
        out_specs=[row, row, row, row, vb_spec],
        out_shape=[jax.ShapeDtypeStruct((n, SB_WIDTH), BF16),
                   jax.ShapeDtypeStruct((n, SB_WIDTH), F32),
                   jax.ShapeDtypeStruct((n, SB_WIDTH), F32),
                   jax.ShapeDtypeStruct((n, SB_WIDTH), BF16),
                   vb_shape],
        compiler_params=_params("parallel"),
        name="sb_proj_t" if transposed else "sb_proj",
    )(x, g, w_in, w_vt)


def _sb_attn_t_kernel(q_ref, k_ref, vt_ref, o_ref, *, tq, groups):
    tk = tq
    qi = pl.program_id(2)
    prev = jnp.maximum(qi - 1, 0)
    has_prev = (qi > 0).astype(F32)
    lane_head = lax.broadcasted_iota(jnp.int32, (tq, LANES), 1) // SB_HEAD_DIM
    group = lambda g: slice(g * LANES, (g + 1) * LANES)
    heads = [(g, hh) for g in range(groups) for hh in range(2)]
    q2 = [q_ref[0, :, group(g)] for g in range(groups)]
    qh = {h: jnp.where(lane_head == h[1], q2[h[0]], jnp.zeros_like(q2[h[0]])) for h in heads}
    key_row = lax.broadcasted_iota(jnp.int32, (tk, tk), 0)
    col = lax.broadcasted_iota(jnp.int32, (tk, tk), 1)
    later = (key_row < col).astype(BF16)
    causal = key_row < col

    def logits(kb, h):
        start = pl.multiple_of(kb * tk, tk)
        return lax.dot_general(k_ref[0, pl.ds(start, tk), group(h[0])], qh[h], NT,
                               preferred_element_type=F32)

    def suffix_in_block(sp):
        return jnp.dot(later, sp.astype(BF16), preferred_element_type=F32)

    def values(kb, h, a):
        return jnp.dot(vt_ref[0, kb, group(h[0]), :], a, preferred_element_type=F32)

    def col_sum(x):
        return jnp.sum(x, axis=0, keepdims=True)

    def window():
        z_diag = {h: logits(qi, h) for h in heads}
        z_prev = {h: logits(prev, h) for h in heads}
        sp_diag = {h: _softplus2(z_diag[h]) for h in heads}
        sp_prev = {h: _softplus2(z_prev[h]) for h in heads}
        spm = {h: jnp.where(causal, sp_diag[h], 0.0) for h in heads}
        suf_diag = {h: suffix_in_block(spm[h]) for h in heads}
        suf_prev = {h: suffix_in_block(sp_prev[h]) for h in heads}
        diag_sum = {h: col_sum(spm[h]) for h in heads}
        a_diag = {h: jnp.where(causal, jnp.exp2(z_diag[h] - sp_diag[h] - suf_diag[h]), 0.0).astype(BF16)
                  for h in heads}
        a_prev = {h: jnp.exp2(z_prev[h] - sp_prev[h] - suf_prev[h] - diag_sum[h]).astype(BF16)
                  for h in heads}
        pv_diag = {h: values(qi, h, a_diag[h]) for h in heads}
        pv_prev = {h: values(prev, h, a_prev[h]) for h in heads}
        carry = {h: diag_sum[h] + has_prev * col_sum(sp_prev[h]) for h in heads}
        return carry, {h: pv_diag[h] + has_prev * pv_prev[h] for h in heads}

    def earlier(kb, hs, carry, acc):
        zt = [logits(kb, h) for h in hs]
        sp = [_softplus2(z) for z in zt]
        suf = [suffix_in_block(s) for s in sp]
        a = [jnp.exp2(zt[i] - sp[i] - suf[i] - carry[i]).astype(BF16) for i in range(len(hs))]
        acc = [acc[i] + values(kb, hs[i], a[i]) for i in range(len(hs))]
        return [carry[i] + col_sum(sp[i]) for i in range(len(hs))], acc

    def unfinished(c):
        return jnp.minimum(jnp.min(c[0]), jnp.min(c[1])) < SB_LOG2_ZERO

    def cond(state):
        return jnp.logical_and(state[0] >= 0, state[1])

    carry, acc = window()
    row_head = lax.broadcasted_iota(jnp.int32, (LANES, tq), 0) // SB_HEAD_DIM
    for g in range(groups):
        hs = [(g, 0), (g, 1)]

        def body(state, hs=hs):
            kb, _, c, a = state
            c, a = earlier(kb, hs, c, a)
            return kb - 1, unfinished(c), c, a

        c0 = [carry[h] for h in hs]
        a0, a1 = lax.while_loop(cond, body, (prev - 1, unfinished(c0), c0, [acc[h] for h in hs]))[3]
        o_ref[0, :, group(g)] = jnp.where(row_head == 0, a0, a1).T.astype(o_ref.dtype)


def _sb_attn_t(q, k, vt, *, tq, groups=2):
    b, s, _ = q.shape
    nkb = vt.shape[1]
    assert s % tq == 0 and nkb * tq == s and vt.shape[3] == tq
    pairs = SB_WIDTH // LANES
    gw = groups * LANES
    return pl.pallas_call(
        functools.partial(_sb_attn_t_kernel, tq=tq, groups=groups),
        grid=(b, pairs // groups, s // tq),
        in_specs=[pl.BlockSpec((1, tq, gw), lambda bi, h, qi: (bi, qi, h)),
                  pl.BlockSpec((1, s, gw), lambda bi, h, qi: (bi, 0, h)),
                  pl.BlockSpec((1, nkb, gw, tq), lambda bi, h, qi: (bi, 0, h, 0))],
        out_specs=pl.BlockSpec((1, tq, gw), lambda bi, h, qi: (bi, qi, h)),
        out_shape=jax.ShapeDtypeStruct((b, s, SB_WIDTH), BF16),
        compiler_params=_params("parallel", "parallel", "arbitrary"),
        name="sb_attn_t",
    )(q, k, vt)


def _sb_attn_kernel(q_ref, kn_ref, vn_ref, kc_ref, vc_ref, o_ref, *, tk):
    sq = q_ref.shape[1]
    rows = SB_HEADS * sq
    n_cache = kc_ref.shape[0] // (SB_HEADS * tk)
    head = lambda h: slice(h * SB_HEAD_DIM, (h + 1) * SB_HEAD_DIM)
    q = [q_ref[0, :, head(h)] for h in range(SB_HEADS)]
    later = (lax.broadcasted_iota(jnp.int32, (tk, tk), 0)
             > lax.broadcasted_iota(jnp.int32, (tk, tk), 1)).astype(BF16)
    pad = jnp.zeros((tk - sq, SB_HEAD_DIM), BF16)

    def new_rows(ref, h):
        return jnp.concatenate([ref[0, :, head(h)], pad], axis=0)

    def cache_rows(ref, kb, h):
        return ref[pl.ds(kb * (tk * SB_HEADS) + h, tk, stride=SB_HEADS), :].astype(BF16)

    def block(keys, vals, carry, acc, masked):
        z = jnp.concatenate([lax.dot_general(q[h], keys(h), NT, preferred_element_type=F32)
                             for h in range(SB_HEADS)], axis=0)
        sp = _softplus2(z)
        spm = sp
        if masked:
            ok = (lax.broadcasted_iota(jnp.int32, (rows, tk), 1)
                  < lax.rem(lax.broadcasted_iota(jnp.int32, (rows, tk), 0), sq))
            spm = jnp.where(ok, sp, 0.0)
        suffix = jnp.dot(spm.astype(BF16), later, preferred_element_type=F32) + carry
        a = jnp.exp2(z - sp - suffix)
        if masked:
            a = jnp.where(ok, a, 0.0)
        a = a.astype(BF16)
        pv = jnp.concatenate([jnp.dot(a[sq * h:sq * (h + 1)], vals(h), preferred_element_type=F32)
                              for h in range(SB_HEADS)], axis=0)
        return carry + jnp.sum(spm, axis=1, keepdims=True), acc + pv

    carry, acc = block(lambda h: new_rows(kn_ref, h), lambda h: new_rows(vn_ref, h),
                       jnp.zeros((rows, 1), F32), jnp.zeros((rows, SB_HEAD_DIM), F32), True)

    def cond(state):
        return jnp.logical_and(state[0] >= 0, state[1])

    def body(state):
        kb, _, carry, acc = state
        carry, acc = block(lambda h: cache_rows(kc_ref, kb, h), lambda h: cache_rows(vc_ref, kb, h),
                           carry, acc, False)
        return kb - 1, jnp.min(carry) < SB_LOG2_ZERO, carry, acc

    acc = lax.while_loop(cond, body, (n_cache - 1, jnp.min(carry) < SB_LOG2_ZERO, carry, acc))[3]
    for h in range(SB_HEADS):
        o_ref[0, :, head(h)] = acc[sq * h:sq * (h + 1)].astype(o_ref.dtype)


def _sb_attn(q, k_new, v_new, k_cache, v_cache, layer, *, tk):
    b, sq, _ = q.shape
    n_layers, _, past, heads, hd = k_cache.shape
    assert past % tk == 0 and sq <= tk and heads == SB_HEADS and hd == SB_HEAD_DIM
    flat = lambda c: c.reshape(n_layers * b * past * heads, hd)
    cache = pl.BlockSpec((past * heads, hd), lambda bi: (layer * b + bi, 0))
    return pl.pallas_call(
        functools.partial(_sb_attn_kernel, tk=tk),
        grid=(b,),
        in_specs=[_stream_block(sq, SB_WIDTH), _stream_block(sq, SB_WIDTH), _stream_block(sq, SB_WIDTH),
                  cache, cache],
        out_specs=_stream_block(sq, SB_WIDTH),
        out_shape=jax.ShapeDtypeStruct((b, sq, SB_WIDTH), BF16),
        compiler_params=_params("parallel"),
        name="sb_attn",
    )(q, k_new, v_new, flat(k_cache), flat(v_cache))


def _post_kernel(x_ref, o_ref, wo_ref, g_ref, wup_ref, wdn_ref, gfin_ref, y_ref, *, final, ff_blk):
    x1 = x_ref[...] + jnp.dot(o_ref[...], wo_ref[...], preferred_element_type=F32)
    h = _rms(x1, g_ref[...]).astype(BF16)
    acc = x1
    for c in range(0, D_FF, ff_blk):
        up = jnp.dot(h, wup_ref[:, c:c + ff_blk], preferred_element_type=F32)
        act = jnp.square(jnp.maximum(up, 0.0)).astype(BF16)
        acc = acc + jnp.dot(act, wdn_ref[c:c + ff_blk, :], preferred_element_type=F32)
    y_ref[...] = _rms(acc, gfin_ref[...]) if final else acc


def _post(x, o, w_o, g_mlp, w_up, w_dn, g_fin, *, final):
    n = x.shape[0]
    tm = _row_block(n)
    row = pl.BlockSpec((tm, D_MODEL), lambda i: (i, 0))
    kern = functools.partial(_post_kernel, final=final, ff_blk=1024)
    return pl.pallas_call(
        kern,
        grid=(n // tm,),
        in_specs=[row, row, _resident(w_o.shape), _resident(g_mlp.shape), _resident(w_up.shape),
                  _resident(w_dn.shape), _resident(g_fin.shape)],
        out_specs=row,
        out_shape=jax.ShapeDtypeStruct((n, D_MODEL), F32),
        compiler_params=_params("parallel"),
        name="post_final" if final else "post",
    )(x, o, w_o, g_mlp, w_up, w_dn, g_fin)


def _cast_kernel(x_ref, o_ref):
    o_ref[...] = x_ref[...].astype(o_ref.dtype)


def _cast_bf16(w):
    cols = w.shape[-1]
    rows = w.size // cols
    tm = rows
    while tm * cols * 4 > CAST_BLOCK_BYTES and tm % 32 == 0:
        tm //= 2
    spec = pl.BlockSpec((tm, cols), lambda i: (i, 0))
    out = pl.pallas_call(
        _cast_kernel, grid=(rows // tm,), in_specs=[spec], out_specs=spec,
        out_shape=jax.ShapeDtypeStruct((rows, cols), BF16),
        compiler_params=_params("parallel"), name="cast_bf16",
    )(w.reshape(rows, cols))
    return out.reshape(w.shape)


def _prep_weights(norm_mix, norm_mlp, norm_final, mla_w_in, mla_g_q, mla_w_qb, mla_g_kv,
                  mla_w_kvb, mla_w_o, sb_w_in, sb_w_o, mlp_w_up, mlp_w_down):
    half = MLA_ROPE // 2
    lat = MLA_Q_LORA + MLA_KV_LORA
    w_in_ext = jnp.concatenate(
        [mla_w_in, -mla_w_in[:, :, lat + half:], mla_w_in[:, :, lat:lat + half]], axis=2)
    n_mla = mla_w_qb.shape[0]
    qb = mla_w_qb.reshape(n_mla, MLA_Q_LORA, MLA_HEADS, MLA_NOPE + MLA_ROPE)
    qb_rot = jnp.concatenate([-qb[..., MLA_NOPE + half:], qb[..., MLA_NOPE:MLA_NOPE + half]], axis=-1)
    w_qb_ext = jnp.concatenate(
        [qb[..., :MLA_NOPE].reshape(n_mla, MLA_Q_LORA, MLA_QN),
         qb[..., MLA_NOPE:].reshape(n_mla, MLA_Q_LORA, MLA_QR),
         qb_rot.reshape(n_mla, MLA_Q_LORA, MLA_QR)], axis=2)
    kvb = mla_w_kvb.reshape(n_mla, MLA_KV_LORA, MLA_HEADS, MLA_NOPE + MLA_V)
    w_k = kvb[..., :MLA_NOPE].reshape(n_mla, MLA_KV_LORA, MLA_QN).astype(BF16)
    w_v = kvb[..., MLA_NOPE:].reshape(n_mla, MLA_KV_LORA, MLA_VW).astype(BF16)
    sb_w_in = _cast_bf16(sb_w_in)
    return dict(
        norm_mix=norm_mix[:, None, :], norm_mlp=norm_mlp[:, None, :], norm_final=norm_final[None, :],
        w_in_ext=w_in_ext.astype(BF16), g_q=mla_g_q[:, None, :], g_kv=mla_g_kv[:, None, :],
        w_qb_ext=w_qb_ext.astype(BF16), w_k=w_k, w_v=w_v, w_vt=jnp.swapaxes(w_v, 1, 2),
        mla_w_o=_cast_bf16(mla_w_o), sb_w_in=sb_w_in,
        sb_w_vt=jnp.swapaxes(sb_w_in[:, :, 2 * SB_WIDTH:], 1, 2), sb_w_o=_cast_bf16(sb_w_o),
        w_up=_cast_bf16(mlp_w_up), w_dn=_cast_bf16(mlp_w_down))


def _rope_tables(pos, reps):
    inv = ROPE_THETA ** (-jnp.arange(0, MLA_ROPE, 2, dtype=F32) / MLA_ROPE)
    ang = pos.astype(F32)[:, None] * inv[None, :]
    tile = lambda t: jnp.tile(t, (reps, LANES // (MLA_ROPE // 2)))
    return tile(jnp.cos(ang)), tile(jnp.sin(ang))


def _rope_key_variants(kr):
    zero = jnp.zeros_like(kr)
    return jnp.stack([jnp.concatenate([kr, zero], axis=2), jnp.concatenate([zero, kr], axis=2)], axis=1)


def _trunk(x, caches, w):
    b, s, _ = x.shape
    n = b * s
    prompt = caches is None
    past = 0 if prompt else caches[0].shape[2]
    if prompt:
        mla_t = _row_block(s, MLA_TILE)
        sb_t = _row_block(s // 2, SB_TILE)
    pos = past + jnp.arange(s, dtype=jnp.int32)
    cos_t, sin_t = _rope_tables(pos, 1 if prompt else b)
    xf = x.reshape(n, D_MODEL)
    ckv_l, kr_l, k_l, v_l = [], [], [], []
    for i in range(DEPTH):
        j = i // 2
        if i % 2 == 0:
            ckv, krope, qn, qr = _mla_proj(xf, w["norm_mix"][i], w["w_in_ext"][j], w["g_q"][j],
                                           w["g_kv"][j], w["w_qb_ext"][j], cos_t, sin_t)
            ckv_l.append(ckv.reshape(b, s, MLA_KV_LORA))
            krope = krope.reshape(b, s, MLA_ROPE)
            kr_l.append(krope)
            qn = qn.reshape(b, s, MLA_QN)
            qr = qr.reshape(b, s, MLA_QR)
            if prompt:
                kn, vt = _mla_kv(ckv, w["w_k"][j], w["w_vt"][j], tm=mla_t, blocks_per_batch=s // mla_t)
                o = _mla_attn_t(qn, qr, kn.reshape(b, s, MLA_QN),
                                _rope_key_variants(krope.astype(BF16)), vt, tq=mla_t, tk=mla_t)
            else:
                o = _mla_attn(qn, qr, ckv_l[-1], krope, caches[0], caches[1], j, w["w_k"][j], w["w_v"][j])
            w_o = w["mla_w_o"][j]
        else:
            if prompt:
                q, k, v, kb, vb = _sb_proj(xf, w["norm_mix"][i], w["sb_w_in"][j], w["sb_w_vt"][j],
                                           tm=sb_t, blocks_per_batch=s // sb_t)
                o = _sb_attn_t(q.reshape(b, s, SB_WIDTH), kb.reshape(b, s, SB_WIDTH), vb, tq=sb_t)
            else:
                q, k, v, kb, vb = _sb_proj(xf, w["norm_mix"][i], w["sb_w_in"][j], w["sb_w_vt"][j],
                                           tm=_row_block(n))
                rows = lambda t: t.reshape(b, s, SB_WIDTH)
                o = _sb_attn(rows(q), rows(kb), rows(vb), caches[2], caches[3], j, tk=LANES)
            k_l.append(k.reshape(b, s, SB_HEADS, SB_HEAD_DIM))
            v_l.append(v.reshape(b, s, SB_HEADS, SB_HEAD_DIM))
            w_o = w["sb_w_o"][j]
        xf = _post(xf, o.reshape(n, D_MODEL), w_o, w["norm_mlp"][i], w["w_up"][i], w["w_dn"][i],
                   w["norm_final"], final=(i == DEPTH - 1))
    return (xf.reshape(b, s, D_MODEL), jnp.stack(ckv_l), jnp.stack(kr_l), jnp.stack(k_l),
            jnp.stack(v_l))


def kernel(x_prompt, x_sample, cache_mla_ckv, cache_mla_krope, cache_sb_k, cache_sb_v, norm_mix, norm_mlp, norm_final, mla_w_in, mla_g_q, mla_w_qb, mla_g_kv, mla_w_kvb, mla_w_o, sb_w_in, sb_w_o, mlp_w_up, mlp_w_down):
    w = _prep_weights(norm_mix, norm_mlp, norm_final, mla_w_in, mla_g_q, mla_w_qb, mla_g_kv,
                      mla_w_kvb, mla_w_o, sb_w_in, sb_w_o, mlp_w_up, mlp_w_down)
    y_p, p_ckv, p_kr, p_k, p_v = _trunk(x_prompt, None, w)
    y_s, s_ckv, s_kr, s_k, s_v = _trunk(
        x_sample, (cache_mla_ckv, cache_mla_krope, cache_sb_k, cache_sb_v), w)
    return (y_p, y_s, p_ckv, p_kr, p_k, p_v, s_ckv, s_kr, s_k, s_v)
```

```python
import functools
import math

import jax
import jax.numpy as jnp
from jax import lax
from jax.experimental import pallas as pl
from jax.experimental.pallas import tpu as pltpu

D_MODEL = 1024
DEPTH = 4
CHUNK = 64
MLA_HEADS = 8
MLA_NOPE = 128
MLA_ROPE = 64
MLA_V = 128
MLA_Q_LORA = 512
MLA_KV_LORA = 256
ROPE_THETA = 10000.0
SB_HEADS = 16
SB_HEAD_DIM = 64
SB_WIDTH = SB_HEADS * SB_HEAD_DIM
D_FF = 4 * D_MODEL
EPS = 1e-6

LANES = 128
MLA_IN_EXT = MLA_Q_LORA + MLA_KV_LORA + 2 * MLA_ROPE
MLA_QN = MLA_HEADS * MLA_NOPE
MLA_QR = MLA_HEADS * MLA_ROPE
MLA_VW = MLA_HEADS * MLA_V
VMEM_LIMIT_BYTES = 56 * 1024 * 1024
MLA_TILE = 512
SB_TILE = 256
CAST_BLOCK_BYTES = 4 * 1024 * 1024
LOG2E = math.log2(math.e)
LN2 = math.log(2.0)

SB_LOG2_ZERO = 151.0

F32 = jnp.float32
BF16 = jnp.bfloat16
NT = (((1,), (1,)), ((), ()))


def _rms(x, g):
    return x * lax.rsqrt(jnp.mean(x * x, axis=-1, keepdims=True) + EPS) * g


def _params(*sem):
    return pltpu.CompilerParams(dimension_semantics=sem, vmem_limit_bytes=VMEM_LIMIT_BYTES)


def _resident(shape):
    nd = len(shape)
    return pl.BlockSpec(shape, lambda *_: (0,) * nd, pipeline_mode=pl.Buffered(1))


def _row_block(n, cap=512):
    for tm in (512, 256, 128, 64, 32, 16, 8):
        if tm <= cap and n % tm == 0:
            return tm
    raise ValueError(f"row count {n} is not a multiple of 8")


def _softplus2(z2):
    return jnp.maximum(z2, 0.0) + jnp.log(1.0 + jnp.exp2(-jnp.abs(z2))) * (1.0 / LN2)


def _mla_proj_kernel(x_ref, gmix_ref, win_ref, gq_ref, gkv_ref, wqb_ref, cos_ref, sin_ref,
                     ckv_ref, krope_ref, kr2_ref, qn_ref, qr_ref):
    scale = (MLA_NOPE + MLA_ROPE) ** -0.5 * LOG2E
    h = _rms(x_ref[...], gmix_ref[...]).astype(BF16)
    a = jnp.dot(h, win_ref[...], preferred_element_type=F32)
    c_q = _rms(a[:, :MLA_Q_LORA], gq_ref[...])
    c_kv = _rms(a[:, MLA_Q_LORA:MLA_Q_LORA + MLA_KV_LORA], gkv_ref[...])
    ckv_ref[...] = c_kv
    cos = cos_ref[...]
    sin = sin_ref[...]
    kr_pair = a[:, MLA_Q_LORA + MLA_KV_LORA:]
    lane = lax.broadcasted_iota(jnp.int32, kr_pair.shape, 1)
    u = kr_pair * jnp.where(lane < MLA_ROPE, cos, sin)
    both = u + pltpu.roll(u, MLA_ROPE, axis=1)
    krope_ref[...] = both[:, :MLA_ROPE]
    kr2_ref[0] = jnp.where(lane < MLA_ROPE, both, 0.0).astype(BF16)
    kr2_ref[1] = jnp.where(lane < MLA_ROPE, 0.0, both).astype(BF16)
    q = jnp.dot(c_q.astype(BF16), wqb_ref[...], preferred_element_type=F32)
    qn_ref[...] = (q[:, :MLA_QN] * scale).astype(BF16)
    cos4 = jnp.concatenate([cos] * (MLA_QR // LANES), axis=1)
    sin4 = jnp.concatenate([sin] * (MLA_QR // LANES), axis=1)
    qr = q[:, MLA_QN:MLA_QN + MLA_QR] * cos4 + q[:, MLA_QN + MLA_QR:] * sin4
    qr_ref[...] = (qr * scale).astype(BF16)


def _mla_proj(x, g_mix, w_in_ext, g_q, g_kv, w_qb_ext, cos_t, sin_t):
    n = x.shape[0]
    tm = min(_row_block(n), _row_block(cos_t.shape[0]))
    n_tab = cos_t.shape[0] // tm
    row = lambda w: pl.BlockSpec((tm, w), lambda i: (i, 0))
    tab = pl.BlockSpec((tm, LANES), lambda i: (i % n_tab, 0))
    return pl.pallas_call(
        _mla_proj_kernel,
        grid=(n // tm,),
        in_specs=[row(D_MODEL), _resident(g_mix.shape), _resident(w_in_ext.shape),
                  _resident(g_q.shape), _resident(g_kv.shape), _resident(w_qb_ext.shape),
                  tab, tab],
        out_specs=[row(MLA_KV_LORA), row(MLA_ROPE), pl.BlockSpec((2, tm, LANES), lambda i: (0, i, 0)),
                   row(MLA_QN), row(MLA_QR)],
        out_shape=[jax.ShapeDtypeStruct((n, MLA_KV_LORA), F32),
                   jax.ShapeDtypeStruct((n, MLA_ROPE), F32),
                   jax.ShapeDtypeStruct((2, n, LANES), BF16),
                   jax.ShapeDtypeStruct((n, MLA_QN), BF16),
                   jax.ShapeDtypeStruct((n, MLA_QR), BF16)],
        compiler_params=_params("parallel"),
        name="mla_proj",
    )(x, g_mix, w_in_ext, g_q, g_kv, w_qb_ext, cos_t, sin_t)


def _mla_kv_kernel(ckv_ref, wk_ref, wvt_ref, kn_ref, vt_ref):
    c = ckv_ref[...].astype(BF16)
    kn_ref[...] = jnp.dot(c, wk_ref[...], preferred_element_type=F32).astype(BF16)
    vt_ref[0, 0] = lax.dot_general(wvt_ref[...], c, NT, preferred_element_type=F32).astype(BF16)


def _mla_kv(ckv, w_k, w_vt, *, tm, blocks_per_batch):
    n = ckv.shape[0]
    nkb = blocks_per_batch
    row = lambda w: pl.BlockSpec((tm, w), lambda i: (i, 0))
    return pl.pallas_call(
        _mla_kv_kernel,
        grid=(n // tm,),
        in_specs=[row(MLA_KV_LORA), _resident(w_k.shape), _resident(w_vt.shape)],
        out_specs=[row(MLA_QN), pl.BlockSpec((1, 1, MLA_VW, tm), lambda i: (i // nkb, i % nkb, 0, 0))],
        out_shape=[jax.ShapeDtypeStruct((n, MLA_QN), BF16),
                   jax.ShapeDtypeStruct((n // (tm * nkb), nkb, MLA_VW, tm), BF16)],
        compiler_params=_params("parallel"),
        name="mla_kv_t",
    )(ckv, w_k, w_vt)


def _mla_attn_t_kernel(qn_ref, qr_ref, kn_ref, kr_ref, vt_ref, o_ref, acc_ref, *, tq, tk):
    qi = pl.program_id(2)
    p0 = qi * tq
    qr = qr_ref[0]
    qs = [jnp.concatenate([qn_ref[0, :, hh * LANES:(hh + 1) * LANES], qr], axis=1)
          for hh in range(2)]
    acc_ref[...] = jnp.zeros_like(acc_ref)

    def scores(kb, hh):
        start = pl.multiple_of(kb * tk, tk)
        k = jnp.concatenate([kn_ref[0, pl.ds(start, tk), hh * LANES:(hh + 1) * LANES],
                             kr_ref[hh, 0, pl.ds(start, tk), :]], axis=1)
        return lax.dot_general(k, qs[hh], NT, preferred_element_type=F32)

    def consume(kb, hh, st, m, l):
        m_new = jnp.maximum(m, jnp.max(st, axis=0, keepdims=True))
        alpha = jnp.exp2(m - m_new)
        pt = jnp.exp2(st - m_new)
        l = alpha * l + jnp.sum(pt, axis=0, keepdims=True)
        pv = jnp.dot(vt_ref[0, kb, hh * MLA_V:(hh + 1) * MLA_V, :], pt.astype(BF16),
                     preferred_element_type=F32)
        acc_ref[hh] = alpha * acc_ref[hh] + pv
        return m_new, l

    k_pos = lax.broadcasted_iota(jnp.int32, (tk, tq), 0)
    q_pos = lax.broadcasted_iota(jnp.int32, (tk, tq), 1)
    visible = k_pos // CHUNK <= q_pos // CHUNK

    def run(base, n, carry, mask_last=False):
        st = [scores(base, hh) for hh in range(2)]
        for i in range(n):
            nxt = [scores(base + i + 1, hh) for hh in range(2)] if i + 1 < n else None
            if mask_last and i == n - 1:
                st = [jnp.where(visible, s, -jnp.inf) for s in st]
            carry = tuple(consume(base + i, hh, st[hh], *carry[hh]) for hh in range(2))
            st = nxt
        return carry

    n_full = p0 // tk
    n_loop = jnp.maximum(n_full - 1, 0)
    init = (jnp.full((1, tq), -1e30, F32), jnp.zeros((1, tq), F32))
    carry = lax.fori_loop(0, n_loop // 4, lambda j, c: run(4 * j, 4, c), (init, init))
    carry = lax.fori_loop(2 * (n_loop // 4), n_loop // 2, lambda j, c: run(2 * j, 2, c), carry)
    carry = lax.fori_loop(2 * (n_loop // 2), n_loop, lambda j, c: run(j, 1, c), carry)
    carry = lax.cond(n_full > 0,
                     lambda c: run(n_full - 1, 2, c, mask_last=True),
                     lambda c: run(n_full, 1, c, mask_last=True), carry)
    for hh in range(2):
        o = acc_ref[hh] * (1.0 / carry[hh][1])
        o_ref[0, :, hh * MLA_V:(hh + 1) * MLA_V] = o.T.astype(o_ref.dtype)


def _mla_attn_t(qn, qr, kn, kr2, vt, *, tq, tk):
    b, s, _ = qn.shape
    nkb = vt.shape[1]
    assert tq == tk and s % tq == 0 and kn.shape[1] == s and nkb * tk == s
    kern = functools.partial(_mla_attn_t_kernel, tq=tq, tk=tk)
    return pl.pallas_call(
        kern,
        grid=(b, MLA_HEADS // 2, s // tq),
        in_specs=[pl.BlockSpec((1, tq, 2 * LANES), lambda bi, hp, qi: (bi, qi, hp)),
                  pl.BlockSpec((1, tq, LANES), lambda bi, hp, qi: (bi, qi, hp)),
                  pl.BlockSpec((1, s, 2 * LANES), lambda bi, hp, qi: (bi, 0, hp)),
                  pl.BlockSpec((2, 1, s, LANES), lambda bi, hp, qi: (0, bi, 0, 0)),
                  pl.BlockSpec((1, nkb, 2 * MLA_V, tk), lambda bi, hp, qi: (bi, 0, hp, 0))],
        out_specs=pl.BlockSpec((1, tq, 2 * MLA_V), lambda bi, hp, qi: (bi, qi, hp)),
        out_shape=jax.ShapeDtypeStruct((b, s, MLA_VW), BF16),
        scratch_shapes=[pltpu.VMEM((2, MLA_V, tq), F32)],
        compiler_params=_params("parallel", "parallel", "arbitrary"),
        name="mla_attn_t",
    )(qn, qr, kn, kr2, vt)


def _mla_attn_kernel(qn_ref, qr_ref, cn_ref, rn_ref, cc_ref, rc_ref, wk_ref, wv_ref, o_ref, *, t_pad):
    sq = qn_ref.shape[1]
    past = cc_ref.shape[2]
    t_valid = past + sq

    def all_rows(cache, new):
        width = new.shape[1]
        return jnp.concatenate([cache.astype(BF16), new.astype(BF16),
                                jnp.zeros((t_pad - t_valid, width), BF16)], axis=0)

    latent = all_rows(cc_ref[0, 0], cn_ref[0])
    rope = all_rows(rc_ref[0, 0], rn_ref[0])
    r_in = lax.broadcasted_iota(jnp.int32, (MLA_ROPE, LANES), 0)
    r_out = lax.broadcasted_iota(jnp.int32, (MLA_ROPE, LANES), 1)
    rope_half = [jnp.dot(rope, (r_out == r_in + hh * MLA_ROPE).astype(BF16),
                         preferred_element_type=F32).astype(BF16) for hh in range(2)]
    q_pos = past + lax.broadcasted_iota(jnp.int32, (sq, t_pad), 0)
    k_pos = lax.broadcasted_iota(jnp.int32, (sq, t_pad), 1)
    visible = (k_pos // CHUNK <= q_pos // CHUNK) & (k_pos < t_valid)
    for pair in range(MLA_HEADS // 2):
        two = slice(pair * 2 * LANES, (pair + 1) * 2 * LANES)
        kn = jnp.dot(latent, wk_ref[:, two], preferred_element_type=F32).astype(BF16)
        v = jnp.dot(latent, wv_ref[:, two], preferred_element_type=F32).astype(BF16)
        qr = qr_ref[0, :, pair * LANES:(pair + 1) * LANES]
        for hh in range(2):
            cols = slice((2 * pair + hh) * LANES, (2 * pair + hh + 1) * LANES)
            half = slice(hh * LANES, (hh + 1) * LANES)
            q = jnp.concatenate([qn_ref[0, :, cols], qr], axis=1)
            k = jnp.concatenate([kn[:, half], rope_half[hh]], axis=1)
            s = jnp.where(visible, lax.dot_general(q, k, NT, preferred_element_type=F32), -jnp.inf)
            p = jnp.exp2(s - jnp.max(s, axis=1, keepdims=True))
            o = jnp.dot(p.astype(BF16), v[:, half], preferred_element_type=F32)
            o_ref[0, :, cols] = (o / jnp.sum(p, axis=1, keepdims=True)).astype(o_ref.dtype)


def _stream_block(*shape):
    return pl.BlockSpec((1,) + shape, lambda bi: (bi,) + (0,) * len(shape))


def _mla_attn(qn, qr, ckv_new, kr_new, ckv_cache, kr_cache, layer, w_k, w_v):
    b, sq, _ = qn.shape
    past = ckv_cache.shape[2]
    t_pad = -(-(past + sq) // LANES) * LANES
    cache = lambda width: pl.BlockSpec((1, 1, past, width), lambda bi: (layer, bi, 0, 0))
    return pl.pallas_call(
        functools.partial(_mla_attn_kernel, t_pad=t_pad),
        grid=(b,),
        in_specs=[_stream_block(sq, MLA_QN), _stream_block(sq, MLA_QR),
                  _stream_block(sq, MLA_KV_LORA), _stream_block(sq, MLA_ROPE),
                  cache(MLA_KV_LORA), cache(MLA_ROPE), _resident(w_k.shape), _resident(w_v.shape)],
        out_specs=_stream_block(sq, MLA_VW),
        out_shape=jax.ShapeDtypeStruct((b, sq, MLA_VW), BF16),
        compiler_params=_params("parallel"),
        name="mla_attn",
    )(qn, qr, ckv_new, kr_new, ckv_cache, kr_cache, w_k, w_v)


def _sb_proj_kernel(x_ref, g_ref, w_ref, wvt_ref, q_ref, k_ref, v_ref, kb_ref, vb_ref, *, transposed):
    h = _rms(x_ref[...], g_ref[...]).astype(BF16)
    qkv = jnp.dot(h, w_ref[...], preferred_element_type=F32)
    q_ref[...] = (qkv[:, :SB_WIDTH] * (SB_HEAD_DIM ** -0.5 * LOG2E)).astype(BF16)
    k = qkv[:, SB_WIDTH:2 * SB_WIDTH]
    v = qkv[:, 2 * SB_WIDTH:]
    k_ref[...] = k
    v_ref[...] = v
    kb_ref[...] = k.astype(BF16)
    if transposed:
        vb_ref[0, 0] = lax.dot_general(wvt_ref[...], h, NT, preferred_element_type=F32).astype(BF16)
    else:
        vb_ref[...] = v.astype(BF16)


def _sb_proj(x, g, w_in, w_vt, *, tm, blocks_per_batch=None):
    n = x.shape[0]
    transposed = blocks_per_batch is not None
    row = pl.BlockSpec((tm, SB_WIDTH), lambda i: (i, 0))
    if transposed:
        nkb = blocks_per_batch
        vb_spec = pl.BlockSpec((1, 1, SB_WIDTH, tm), lambda i: (i // nkb, i % nkb, 0, 0))
        vb_shape = jax.ShapeDtypeStruct((n // (tm * nkb), nkb, SB_WIDTH, tm), BF16)
    else:
        vb_spec, vb_shape = row, jax.ShapeDtypeStruct((n, SB_WIDTH), BF16)
    return pl.pallas_call(
        functools.partial(_sb_proj_kernel, transposed=transposed),
        grid=(n // tm,),
        in_specs=[pl.BlockSpec((tm, D_MODEL), lambda i: (i, 0)), _resident(g.shape),
                  _resident(w_in.shape), _resident(w_vt.shape)],
        out_specs=[row, row, row, row, vb_spec],
        out_shape=[jax.ShapeDtypeStruct((n, SB_WIDTH), BF16),
                   jax.ShapeDtypeStruct((n, SB_WIDTH), F32),
                   jax.ShapeDtypeStruct((n, SB_WIDTH), F32),
                   jax.ShapeDtypeStruct((n, SB_WIDTH), BF16),
                   vb_shape],
        compiler_params=_params("parallel"),
        name="sb_proj_t" if transposed else "sb_proj",
    )(x, g, w_in, w_vt)


def _sb_attn_t_kernel(q_ref, k_ref, vt_ref, o_ref, *, tq, groups):
    tk = tq
    qi = pl.program_id(2)
    prev = jnp.maximum(qi - 1, 0)
    has_prev = (qi > 0).astype(F32)
    lane_head = lax.broadcasted_iota(jnp.int32, (tq, LANES), 1) // SB_HEAD_DIM
    group = lambda g: slice(g * LANES, (g + 1) * LANES)
    heads = [(g, hh) for g in range(groups) for hh in range(2)]
    q2 = [q_ref[0, :, group(g)] for g in range(groups)]
    qh = {h: jnp.where(lane_head == h[1], q2[h[0]], jnp.zeros_like(q2[h[0]])) for h in heads}
    key_row = lax.broadcasted_iota(jnp.int32, (tk, tk), 0)
    col = lax.broadcasted_iota(jnp.int32, (tk, tk), 1)
    later = (key_row < col).astype(BF16)
    causal = key_row < col

    def logits(kb, h):
        start = pl.multiple_of(kb * tk, tk)
        return lax.dot_general(k_ref[0, pl.ds(start, tk), group(h[0])], qh[h], NT,
                               preferred_element_type=F32)

    def suffix_in_block(sp):
        return jnp.dot(later, sp.astype(BF16), preferred_element_type=F32)

    def values(kb, h, a):
        return jnp.dot(vt_ref[0, kb, group(h[0]), :], a, preferred_element_type=F32)

    def col_sum(x):
        return jnp.sum(x, axis=0, keepdims=True)

    def window():
        z_diag = {h: logits(qi, h) for h in heads}
        z_prev = {h: logits(prev, h) for h in heads}
        sp_diag = {h: _softplus2(z_diag[h]) for h in heads}
        sp_prev = {h: _softplus2(z_prev[h]) for h in heads}
        spm = {h: jnp.where(causal, sp_diag[h], 0.0) for h in heads}
        suf_diag = {h: suffix_in_block(spm[h]) for h in heads}
        suf_prev = {h: suffix_in_block(sp_prev[h]) for h in heads}
        diag_sum = {h: col_sum(spm[h]) for h in heads}
        a_diag = {h: jnp.where(causal, jnp.exp2(z_diag[h] - sp_diag[h] - suf_diag[h]), 0.0).astype(BF16)
                  for h in heads}
        a_prev = {h: jnp.exp2(z_prev[h] - sp_prev[h] - suf_prev[h] - diag_sum[h]).astype(BF16)
                  for h in heads}
        pv_diag = {h: values(qi, h, a_diag[h]) for h in heads}
        pv_prev = {h: values(prev, h, a_prev[h]) for h in heads}
        carry = {h: diag_sum[h] + has_prev * col_sum(sp_prev[h]) for h in heads}
        return carry, {h: pv_diag[h] + has_prev * pv_prev[h] for h in heads}

    def earlier(kb, hs, carry, acc):
        zt = [logits(kb, h) for h in hs]
        sp = [_softplus2(z) for z in zt]
        suf = [suffix_in_block(s) for s in sp]
        a = [jnp.exp2(zt[i] - sp[i] - suf[i] - carry[i]).astype(BF16) for i in range(len(hs))]
        acc = [acc[i] + values(kb, hs[i], a[i]) for i in range(len(hs))]
        return [carry[i] + col_sum(sp[i]) for i in range(len(hs))], acc

    def unfinished(c):
        return jnp.minimum(jnp.min(c[0]), jnp.min(c[1])) < SB_LOG2_ZERO

    def cond(state):
        return jnp.logical_and(state[0] >= 0, state[1])

    carry, acc = window()
    row_head = lax.broadcasted_iota(jnp.int32, (LANES, tq), 0) // SB_HEAD_DIM
    for g in range(groups):
        hs = [(g, 0), (g, 1)]

        def body(state, hs=hs):
            kb, _, c, a = state
            c, a = earlier(kb, hs, c, a)
            return kb - 1, unfinished(c), c, a

        c0 = [carry[h] for h in hs]
        a0, a1 = lax.while_loop(cond, body, (prev - 1, unfinished(c0), c0, [acc[h] for h in hs]))[3]
        o_ref[0, :, group(g)] = jnp.where(row_head == 0, a0, a1).T.astype(o_ref.dtype)


def _sb_attn_t(q, k, vt, *, tq, groups=2):
    b, s, _ = q.shape
    nkb = vt.shape[1]
    assert s % tq == 0 and nkb * tq == s and vt.shape[3] == tq
    pairs = SB_WIDTH // LANES
    gw = groups * LANES
    return pl.pallas_call(
        functools.partial(_sb_attn_t_kernel, tq=tq, groups=groups),
        grid=(b, pairs // groups, s // tq),
        in_specs=[pl.BlockSpec((1, tq, gw), lambda bi, h, qi: (bi, qi, h)),
                  pl.BlockSpec((1, s, gw), lambda bi, h, qi: (bi, 0, h)),
                  pl.BlockSpec((1, nkb, gw, tq), lambda bi, h, qi: (bi, 0, h, 0))],
        out_specs=pl.BlockSpec((1, tq, gw), lambda bi, h, qi: (bi, qi, h)),
        out_shape=jax.ShapeDtypeStruct((b, s, SB_WIDTH), BF16),
        compiler_params=_params("parallel", "parallel", "arbitrary"),
        name="sb_attn_t",
    )(q, k, vt)


def _sb_attn_kernel(q_ref, kn_ref, vn_ref, kc_ref, vc_ref, o_ref, *, tk):
    sq = q_ref.shape[1]
    rows = SB_HEADS * sq
    n_cache = kc_ref.shape[2] // tk
    head = lambda h: slice(h * SB_HEAD_DIM, (h + 1) * SB_HEAD_DIM)
    q = [q_ref[0, :, head(h)] for h in range(SB_HEADS)]
    later = (lax.broadcasted_iota(jnp.int32, (tk, tk), 0)
             > lax.broadcasted_iota(jnp.int32, (tk, tk), 1)).astype(BF16)
    pad = jnp.zeros((tk - sq, SB_HEAD_DIM), BF16)

    def new_rows(ref, h):
        return jnp.concatenate([ref[0, :, head(h)], pad], axis=0)

    def cache_rows(ref, kb, h):
        return ref[0, 0, pl.ds(kb * tk, tk), h, :].astype(BF16)

    def block(keys, vals, carry, acc, masked):
        z = jnp.concatenate([lax.dot_general(q[h], keys(h), NT, preferred_element_type=F32)
                             for h in range(SB_HEADS)], axis=0)
        sp = _softplus2(z)
        spm = sp
        if masked:
            ok = (lax.broadcasted_iota(jnp.int32, (rows, tk), 1)
                  < lax.rem(lax.broadcasted_iota(jnp.int32, (rows, tk), 0), sq))
            spm = jnp.where(ok, sp, 0.0)
        suffix = jnp.dot(spm.astype(BF16), later, preferred_element_type=F32) + carry
        a = jnp.exp2(z - sp - suffix)
        if masked:
            a = jnp.where(ok, a, 0.0)
        a = a.astype(BF16)
        pv = jnp.concatenate([jnp.dot(a[sq * h:sq * (h + 1)], vals(h), preferred_element_type=F32)
                              for h in range(SB_HEADS)], axis=0)
        return carry + jnp.sum(spm, axis=1, keepdims=True), acc + pv

    carry, acc = block(lambda h: new_rows(kn_ref, h), lambda h: new_rows(vn_ref, h),
                       jnp.zeros((rows, 1), F32), jnp.zeros((rows, SB_HEAD_DIM), F32), True)

    def cond(state):
        return jnp.logical_and(state[0] >= 0, state[1])

    def body(state):
        kb, _, carry, acc = state
        carry, acc = block(lambda h: cache_rows(kc_ref, kb, h), lambda h: cache_rows(vc_ref, kb, h),
                           carry, acc, False)
        return kb - 1, jnp.min(carry) < SB_LOG2_ZERO, carry, acc

    acc = lax.while_loop(cond, body, (n_cache - 1, jnp.min(carry) < SB_LOG2_ZERO, carry, acc))[3]
    for h in range(SB_HEADS):
        o_ref[0, :, head(h)] = acc[sq * h:sq * (h + 1)].astype(o_ref.dtype)


def _sb_attn(q, k_new, v_new, k_cache, v_cache, layer, *, tk):
    b, sq, _ = q.shape
    n_layers, _, past, heads, hd = k_cache.shape
    assert past % tk == 0 and sq <= tk and heads == SB_HEADS and hd == SB_HEAD_DIM
    cache = pl.BlockSpec((1, 1, past, heads, hd), lambda bi: (layer, bi, 0, 0, 0))
    return pl.pallas_call(
        functools.partial(_sb_attn_kernel, tk=tk),
        grid=(b,),
        in_specs=[_stream_block(sq, SB_WIDTH), _stream_block(sq, SB_WIDTH), _stream_block(sq, SB_WIDTH),
                  cache, cache],
        out_specs=_stream_block(sq, SB_WIDTH),
        out_shape=jax.ShapeDtypeStruct((b, sq, SB_WIDTH), BF16),
        compiler_params=_params("parallel"),
        name="sb_attn",
    )(q, k_new, v_new, k_cache, v_cache)


def _post_kernel(x_ref, o_ref, wo_ref, g_ref, wup_ref, wdn_ref, gfin_ref, y_ref, *, final, ff_blk):
    x1 = x_ref[...] + jnp.dot(o_ref[...], wo_ref[...], preferred_element_type=F32)
    h = _rms(x1, g_ref[...]).astype(BF16)
    acc = x1
    for c in range(0, D_FF, ff_blk):
        up = jnp.dot(h, wup_ref[:, c:c + ff_blk], preferred_element_type=F32)
        act = jnp.square(jnp.maximum(up, 0.0)).astype(BF16)
        acc = acc + jnp.dot(act, wdn_ref[c:c + ff_blk, :], preferred_element_type=F32)
    y_ref[...] = _rms(acc, gfin_ref[...]) if final else acc


def _post(x, o, w_o, g_mlp, w_up, w_dn, g_fin, *, final):
    n = x.shape[0]
    tm = _row_block(n)
    row = pl.BlockSpec((tm, D_MODEL), lambda i: (i, 0))
    kern = functools.partial(_post_kernel, final=final, ff_blk=1024)
    return pl.pallas_call(
        kern,
        grid=(n // tm,),
        in_specs=[row, row, _resident(w_o.shape), _resident(g_mlp.shape), _resident(w_up.shape),
                  _resident(w_dn.shape), _resident(g_fin.shape)],
        out_specs=row,
        out_shape=jax.ShapeDtypeStruct((n, D_MODEL), F32),
        compiler_params=_params("parallel"),
        name="post_final" if final else "post",
    )(x, o, w_o, g_mlp, w_up, w_dn, g_fin)


def _cast_kernel(x_ref, o_ref):
    o_ref[...] = x_ref[...].astype(o_ref.dtype)


def _cast_bf16(w):
    cols = w.shape[-1]
    rows = w.size // cols
    tm = rows
    while tm * cols * 4 > CAST_BLOCK_BYTES and tm % 32 == 0:
        tm //= 2
    spec = pl.BlockSpec((tm, cols), lambda i: (i, 0))
    out = pl.pallas_call(
        _cast_kernel, grid=(rows // tm,), in_specs=[spec], out_specs=spec,
        out_shape=jax.ShapeDtypeStruct((rows, cols), BF16),
        compiler_params=_params("parallel"), name="cast_bf16",
    )(w.reshape(rows, cols))
    return out.reshape(w.shape)


def _prep_weights(norm_mix, norm_mlp, norm_final, mla_w_in, mla_g_q, mla_w_qb, mla_g_kv,
                  mla_w_kvb, mla_w_o, sb_w_in, sb_w_o, mlp_w_up, mlp_w_down):
    half = MLA_ROPE // 2
    lat = MLA_Q_LORA + MLA_KV_LORA
    w_in_ext = jnp.concatenate(
        [mla_w_in, -mla_w_in[:, :, lat + half:], mla_w_in[:, :, lat:lat + half]], axis=2)
    n_mla = mla_w_qb.shape[0]
    qb = mla_w_qb.reshape(n_mla, MLA_Q_LORA, MLA_HEADS, MLA_NOPE + MLA_ROPE)
    qb_rot = jnp.concatenate([-qb[..., MLA_NOPE + half:], qb[..., MLA_NOPE:MLA_NOPE + half]], axis=-1)
    w_qb_ext = jnp.concatenate(
        [qb[..., :MLA_NOPE].reshape(n_mla, MLA_Q_LORA, MLA_QN),
         qb[..., MLA_NOPE:].reshape(n_mla, MLA_Q_LORA, MLA_QR),
         qb_rot.reshape(n_mla, MLA_Q_LORA, MLA_QR)], axis=2)
    kvb = mla_w_kvb.reshape(n_mla, MLA_KV_LORA, MLA_HEADS, MLA_NOPE + MLA_V)
    w_k = kvb[..., :MLA_NOPE].reshape(n_mla, MLA_KV_LORA, MLA_QN).astype(BF16)
    w_v = kvb[..., MLA_NOPE:].reshape(n_mla, MLA_KV_LORA, MLA_VW).astype(BF16)
    sb_w_in = _cast_bf16(sb_w_in)
    return dict(
        norm_mix=norm_mix[:, None, :], norm_mlp=norm_mlp[:, None, :], norm_final=norm_final[None, :],
        w_in_ext=w_in_ext.astype(BF16), g_q=mla_g_q[:, None, :], g_kv=mla_g_kv[:, None, :],
        w_qb_ext=w_qb_ext.astype(BF16), w_k=w_k, w_v=w_v, w_vt=jnp.swapaxes(w_v, 1, 2),
        mla_w_o=_cast_bf16(mla_w_o), sb_w_in=sb_w_in,
        sb_w_vt=jnp.swapaxes(sb_w_in[:, :, 2 * SB_WIDTH:], 1, 2), sb_w_o=_cast_bf16(sb_w_o),
        w_up=_cast_bf16(mlp_w_up), w_dn=_cast_bf16(mlp_w_down))


def _rope_tables(pos, reps):
    inv = ROPE_THETA ** (-jnp.arange(0, MLA_ROPE, 2, dtype=F32) / MLA_ROPE)
    ang = pos.astype(F32)[:, None] * inv[None, :]
    tile = lambda t: jnp.tile(t, (reps, LANES // (MLA_ROPE // 2)))
    return tile(jnp.cos(ang)), tile(jnp.sin(ang))


def _trunk(x, caches, w):
    b, s, _ = x.shape
    n = b * s
    prompt = caches is None
    past = 0 if prompt else caches[0].shape[2]
    if prompt:
        mla_t = _row_block(s, MLA_TILE)
        sb_t = _row_block(s // 2, SB_TILE)
    pos = past + jnp.arange(s, dtype=jnp.int32)
    cos_t, sin_t = _rope_tables(pos, 1 if prompt else b)
    xf = x.reshape(n, D_MODEL)
    ckv_l, kr_l, k_l, v_l = [], [], [], []
    for i in range(DEPTH):
        j = i // 2
        if i % 2 == 0:
            ckv, krope, kr2, qn, qr = _mla_proj(xf, w["norm_mix"][i], w["w_in_ext"][j], w["g_q"][j],
                                                w["g_kv"][j], w["w_qb_ext"][j], cos_t, sin_t)
            ckv_l.append(ckv.reshape(b, s, MLA_KV_LORA))
            krope = krope.reshape(b, s, MLA_ROPE)
            kr_l.append(krope)
            qn = qn.reshape(b, s, MLA_QN)
            qr = qr.reshape(b, s, MLA_QR)
            if prompt:
                kn, vt = _mla_kv(ckv, w["w_k"][j], w["w_vt"][j], tm=mla_t, blocks_per_batch=s // mla_t)
                o = _mla_attn_t(qn, qr, kn.reshape(b, s, MLA_QN), kr2.reshape(2, b, s, LANES), vt,
                                tq=mla_t, tk=mla_t)
            else:
                o = _mla_attn(qn, qr, ckv_l[-1], krope, caches[0], caches[1], j, w["w_k"][j], w["w_v"][j])
            w_o = w["mla_w_o"][j]
        else:
            if prompt:
                q, k, v, kb, vb = _sb_proj(xf, w["norm_mix"][i], w["sb_w_in"][j], w["sb_w_vt"][j],
                                           tm=sb_t, blocks_per_batch=s // sb_t)
                o = _sb_attn_t(q.reshape(b, s, SB_WIDTH), kb.reshape(b, s, SB_WIDTH), vb, tq=sb_t)
            else:
                q, k, v, kb, vb = _sb_proj(xf, w["norm_mix"][i], w["sb_w_in"][j], w["sb_w_vt"][j],
                                           tm=_row_block(n))
                rows = lambda t: t.reshape(b, s, SB_WIDTH)
                o = _sb_attn(rows(q), rows(kb), rows(vb), caches[2], caches[3], j, tk=LANES)
            k_l.append(k.reshape(b, s, SB_HEADS, SB_HEAD_DIM))
            v_l.append(v.reshape(b, s, SB_HEADS, SB_HEAD_DIM))
            w_o = w["sb_w_o"][j]
        xf = _post(xf, o.reshape(n, D_MODEL), w_o, w["norm_mlp"][i], w["w_up"][i], w["w_dn"][i],
                   w["norm_final"], final=(i == DEPTH - 1))
    return (xf.reshape(b, s, D_MODEL), jnp.stack(ckv_l), jnp.stack(kr_l), jnp.stack(k_l),
            jnp.stack(v_l))


def kernel(x_prompt, x_sample, cache_mla_ckv, cache_mla_krope, cache_sb_k, cache_sb_v, norm_mix, norm_mlp, norm_final, mla_w_in, mla_g_q, mla_w_qb, mla_g_kv, mla_w_kvb, mla_w_o, sb_w_in, sb_w_o, mlp_w_up, mlp_w_down):
    w = _prep_weights(norm_mix, norm_mlp, norm_final, mla_w_in, mla_g_q, mla_w_qb, mla_g_kv,
                      mla_w_kvb, mla_w_o, sb_w_in, sb_w_o, mlp_w_up, mlp_w_down)
    y_p, p_ckv, p_kr, p_k, p_v = _trunk(x_prompt, None, w)
    y_s, s_ckv, s_kr, s_k, s_v = _trunk(
        x_sample, (cache_mla_ckv, cache_mla_krope, cache_sb_k, cache_sb_v), w)
    return (y_p, y_s, p_ckv, p_kr, p_k, p_v, s_ckv, s_kr, s_k, s_v)
```

```python
import functools
import math

import jax
import jax.numpy as jnp
from jax import lax
from jax.experimental import pallas as pl
from jax.experimental.pallas import tpu as pltpu

D_MODEL = 1024
DEPTH = 4
CHUNK = 64
MLA_HEADS = 8
MLA_NOPE = 128
MLA_ROPE = 64
MLA_V = 128
MLA_Q_LORA = 512
MLA_KV_LORA = 256
ROPE_THETA = 10000.0
SB_HEADS = 16
SB_HEAD_DIM = 64
SB_WIDTH = SB_HEADS * SB_HEAD_DIM
D_FF = 4 * D_MODEL
EPS = 1e-6

LANES = 128
MLA_IN_EXT = MLA_Q_LORA + MLA_KV_LORA + 2 * MLA_ROPE
MLA_QN = MLA_HEADS * MLA_NOPE
MLA_QR = MLA_HEADS * MLA_ROPE
MLA_VW = MLA_HEADS * MLA_V
VMEM_LIMIT_BYTES = 56 * 1024 * 1024
MLA_TILE = 512
SB_TILE = 256
CAST_BLOCK_BYTES = 4 * 1024 * 1024
LOG2E = math.log2(math.e)
LN2 = math.log(2.0)

SB_LOG2_ZERO = 151.0

F32 = jnp.float32
BF16 = jnp.bfloat16
NT = (((1,), (1,)), ((), ()))


def _rms(x, g):
    return x * lax.rsqrt(jnp.mean(x * x, axis=-1, keepdims=True) + EPS) * g


def _params(*sem):
    return pltpu.CompilerParams(dimension_semantics=sem, vmem_limit_bytes=VMEM_LIMIT_BYTES)


def _resident(shape):
    nd = len(shape)
    return pl.BlockSpec(shape, lambda *_: (0,) * nd, pipeline_mode=pl.Buffered(1))


def _row_block(n, cap=512):
    for tm in (512, 256, 128, 64, 32, 16, 8):
        if tm <= cap and n % tm == 0:
            return tm
    raise ValueError(f"row count {n} is not a multiple of 8")


def _softplus2(z2):
    return jnp.maximum(z2, 0.0) + jnp.log(1.0 + jnp.exp2(-jnp.abs(z2))) * (1.0 / LN2)


def _mla_proj_kernel(x_ref, gmix_ref, win_ref, gq_ref, gkv_ref, wqb_ref, cos_ref, sin_ref,
                     ckv_ref, krope_ref, kr2_ref, qn_ref, qr_ref):
    scale = (MLA_NOPE + MLA_ROPE) ** -0.5 * LOG2E
    h = _rms(x_ref[...], gmix_ref[...]).astype(BF16)
    a = jnp.dot(h, win_ref[...], preferred_element_type=F32)
    c_q = _rms(a[:, :MLA_Q_LORA], gq_ref[...])
    c_kv = _rms(a[:, MLA_Q_LORA:MLA_Q_LORA + MLA_KV_LORA], gkv_ref[...])
    ckv_ref[...] = c_kv
    cos = cos_ref[...]
    sin = sin_ref[...]
    kr_pair = a[:, MLA_Q_LORA + MLA_KV_LORA:]
    lane = lax.broadcasted_iota(jnp.int32, kr_pair.shape, 1)
    u = kr_pair * jnp.where(lane < MLA_ROPE, cos, sin)
    both = u + pltpu.roll(u, MLA_ROPE, axis=1)
    krope_ref[...] = both[:, :MLA_ROPE]
    kr2_ref[0] = jnp.where(lane < MLA_ROPE, both, 0.0).astype(BF16)
    kr2_ref[1] = jnp.where(lane < MLA_ROPE, 0.0, both).astype(BF16)
    q = jnp.dot(c_q.astype(BF16), wqb_ref[...], preferred_element_type=F32)
    qn_ref[...] = (q[:, :MLA_QN] * scale).astype(BF16)
    cos4 = jnp.concatenate([cos] * (MLA_QR // LANES), axis=1)
    sin4 = jnp.concatenate([sin] * (MLA_QR // LANES), axis=1)
    qr = q[:, MLA_QN:MLA_QN + MLA_QR] * cos4 + q[:, MLA_QN + MLA_QR:] * sin4
    qr_ref[...] = (qr * scale).astype(BF16)


def _mla_proj(x, g_mix, w_in_ext, g_q, g_kv, w_qb_ext, cos_t, sin_t):
    n = x.shape[0]
    tm = min(_row_block(n), _row_block(cos_t.shape[0]))
    n_tab = cos_t.shape[0] // tm
    row = lambda w: pl.BlockSpec((tm, w), lambda i: (i, 0))
    tab = pl.BlockSpec((tm, LANES), lambda i: (i % n_tab, 0))
    return pl.pallas_call(
        _mla_proj_kernel,
        grid=(n // tm,),
        in_specs=[row(D_MODEL), _resident(g_mix.shape), _resident(w_in_ext.shape),
                  _resident(g_q.shape), _resident(g_kv.shape), _resident(w_qb_ext.shape),
                  tab, tab],
        out_specs=[row(MLA_KV_LORA), row(MLA_ROPE), pl.BlockSpec((2, tm, LANES), lambda i: (0, i, 0)),
                   row(MLA_QN), row(MLA_QR)],
        out_shape=[jax.ShapeDtypeStruct((n, MLA_KV_LORA), F32),
                   jax.ShapeDtypeStruct((n, MLA_ROPE), F32),
                   jax.ShapeDtypeStruct((2, n, LANES), BF16),
                   jax.ShapeDtypeStruct((n, MLA_QN), BF16),
                   jax.ShapeDtypeStruct((n, MLA_QR), BF16)],
        compiler_params=_params("parallel"),
        name="mla_proj",
    )(x, g_mix, w_in_ext, g_q, g_kv, w_qb_ext, cos_t, sin_t)


def _mla_kv_kernel(ckv_ref, wk_ref, wvt_ref, kn_ref, vt_ref):
    c = ckv_ref[...].astype(BF16)
    kn_ref[...] = jnp.dot(c, wk_ref[...], preferred_element_type=F32).astype(BF16)
    vt_ref[0, 0] = lax.dot_general(wvt_ref[...], c, NT, preferred_element_type=F32).astype(BF16)


def _mla_kv(ckv, w_k, w_vt, *, tm, blocks_per_batch):
    n = ckv.shape[0]
    nkb = blocks_per_batch
    row = lambda w: pl.BlockSpec((tm, w), lambda i: (i, 0))
    return pl.pallas_call(
        _mla_kv_kernel,
        grid=(n // tm,),
        in_specs=[row(MLA_KV_LORA), _resident(w_k.shape), _resident(w_vt.shape)],
        out_specs=[row(MLA_QN), pl.BlockSpec((1, 1, MLA_VW, tm), lambda i: (i // nkb, i % nkb, 0, 0))],
        out_shape=[jax.ShapeDtypeStruct((n, MLA_QN), BF16),
                   jax.ShapeDtypeStruct((n // (tm * nkb), nkb, MLA_VW, tm), BF16)],
        compiler_params=_params("parallel"),
        name="mla_kv_t",
    )(ckv, w_k, w_vt)


def _mla_attn_t_kernel(qn_ref, qr_ref, kn_ref, kr_ref, vt_ref, o_ref, acc_ref, *, tq, tk):
    qi = pl.program_id(2)
    p0 = qi * tq
    qr = qr_ref[0]
    qs = [jnp.concatenate([qn_ref[0, :, hh * LANES:(hh + 1) * LANES], qr], axis=1)
          for hh in range(2)]
    acc_ref[...] = jnp.zeros_like(acc_ref)

    def scores(kb, hh):
        start = pl.multiple_of(kb * tk, tk)
        k = jnp.concatenate([kn_ref[0, pl.ds(start, tk), hh * LANES:(hh + 1) * LANES],
                             kr_ref[hh, 0, pl.ds(start, tk), :]], axis=1)
        return lax.dot_general(k, qs[hh], NT, preferred_element_type=F32)

    def consume(kb, hh, st, m, l):
        m_new = jnp.maximum(m, jnp.max(st, axis=0, keepdims=True))
        alpha = jnp.exp2(m - m_new)
        pt = jnp.exp2(st - m_new)
        l = alpha * l + jnp.sum(pt, axis=0, keepdims=True)
        pv = jnp.dot(vt_ref[0, kb, hh * MLA_V:(hh + 1) * MLA_V, :], pt.astype(BF16),
                     preferred_element_type=F32)
        acc_ref[hh] = alpha * acc_ref[hh] + pv
        return m_new, l

    k_pos = lax.broadcasted_iota(jnp.int32, (tk, tq), 0)
    q_pos = lax.broadcasted_iota(jnp.int32, (tk, tq), 1)
    visible = k_pos // CHUNK <= q_pos // CHUNK

    def run(base, n, carry, mask_last=False):
        st = [scores(base, hh) for hh in range(2)]
        for i in range(n):
            nxt = [scores(base + i + 1, hh) for hh in range(2)] if i + 1 < n else None
            if mask_last and i == n - 1:
                st = [jnp.where(visible, s, -jnp.inf) for s in st]
            carry = tuple(consume(base + i, hh, st[hh], *carry[hh]) for hh in range(2))
            st = nxt
        return carry

    n_full = p0 // tk
    n_loop = jnp.maximum(n_full - 1, 0)
    init = (jnp.full((1, tq), -1e30, F32), jnp.zeros((1, tq), F32))
    carry = lax.fori_loop(0, n_loop // 4, lambda j, c: run(4 * j, 4, c), (init, init))
    carry = lax.fori_loop(2 * (n_loop // 4), n_loop // 2, lambda j, c: run(2 * j, 2, c), carry)
    carry = lax.fori_loop(2 * (n_loop // 2), n_loop, lambda j, c: run(j, 1, c), carry)
    carry = lax.cond(n_full > 0,
                     lambda c: run(n_full - 1, 2, c, mask_last=True),
                     lambda c: run(n_full, 1, c, mask_last=True), carry)
    for hh in range(2):
        o = acc_ref[hh] * (1.0 / carry[hh][1])
        o_ref[0, :, hh * MLA_V:(hh + 1) * MLA_V] = o.T.astype(o_ref.dtype)


def _mla_attn_t(qn, qr, kn, kr2, vt, *, tq, tk):
    b, s, _ = qn.shape
    nkb = vt.shape[1]
    assert tq == tk and s % tq == 0 and kn.shape[1] == s and nkb * tk == s
    kern = functools.partial(_mla_attn_t_kernel, tq=tq, tk=tk)
    return pl.pallas_call(
        kern,
        grid=(b, MLA_HEADS // 2, s // tq),
        in_specs=[pl.BlockSpec((1, tq, 2 * LANES), lambda bi, hp, qi: (bi, qi, hp)),
                  pl.BlockSpec((1, tq, LANES), lambda bi, hp, qi: (bi, qi, hp)),
                  pl.BlockSpec((1, s, 2 * LANES), lambda bi, hp, qi: (bi, 0, hp)),
                  pl.BlockSpec((2, 1, s, LANES), lambda bi, hp, qi: (0, bi, 0, 0)),
                  pl.BlockSpec((1, nkb, 2 * MLA_V, tk), lambda bi, hp, qi: (bi, 0, hp, 0))],
        out_specs=pl.BlockSpec((1, tq, 2 * MLA_V), lambda bi, hp, qi: (bi, qi, hp)),
        out_shape=jax.ShapeDtypeStruct((b, s, MLA_VW), BF16),
        scratch_shapes=[pltpu.VMEM((2, MLA_V, tq), F32)],
        compiler_params=_params("parallel", "parallel", "arbitrary"),
        name="mla_attn_t",
    )(qn, qr, kn, kr2, vt)


def _mla_attn_kernel(qn_ref, qr_ref, cn_ref, rn_ref, cc_ref, rc_ref, wk_ref, wv_ref, o_ref, *, t_pad):
    sq = qn_ref.shape[1]
    past = cc_ref.shape[2]
    t_valid = past + sq

    def all_rows(cache, new):
        width = new.shape[1]
        return jnp.concatenate([cache.astype(BF16), new.astype(BF16),
                                jnp.zeros((t_pad - t_valid, width), BF16)], axis=0)

    latent = all_rows(cc_ref[0, 0], cn_ref[0])
    rope = all_rows(rc_ref[0, 0], rn_ref[0])
    r_in = lax.broadcasted_iota(jnp.int32, (MLA_ROPE, LANES), 0)
    r_out = lax.broadcasted_iota(jnp.int32, (MLA_ROPE, LANES), 1)
    rope_half = [jnp.dot(rope, (r_out == r_in + hh * MLA_ROPE).astype(BF16),
                         preferred_element_type=F32).astype(BF16) for hh in range(2)]
    q_pos = past + lax.broadcasted_iota(jnp.int32, (sq, t_pad), 0)
    k_pos = lax.broadcasted_iota(jnp.int32, (sq, t_pad), 1)
    visible = (k_pos // CHUNK <= q_pos // CHUNK) & (k_pos < t_valid)
    for pair in range(MLA_HEADS // 2):
        two = slice(pair * 2 * LANES, (pair + 1) * 2 * LANES)
        kn = jnp.dot(latent, wk_ref[:, two], preferred_element_type=F32).astype(BF16)
        v = jnp.dot(latent, wv_ref[:, two], preferred_element_type=F32).astype(BF16)
        qr = qr_ref[0, :, pair * LANES:(pair + 1) * LANES]
        for hh in range(2):
            cols = slice((2 * pair + hh) * LANES, (2 * pair + hh + 1) * LANES)
            half = slice(hh * LANES, (hh + 1) * LANES)
            q = jnp.concatenate([qn_ref[0, :, cols], qr], axis=1)
            k = jnp.concatenate([kn[:, half], rope_half[hh]], axis=1)
            s = jnp.where(visible, lax.dot_general(q, k, NT, preferred_element_type=F32), -jnp.inf)
            p = jnp.exp2(s - jnp.max(s, axis=1, keepdims=True))
            o = jnp.dot(p.astype(BF16), v[:, half], preferred_element_type=F32)
            o_ref[0, :, cols] = (o / jnp.sum(p, axis=1, keepdims=True)).astype(o_ref.dtype)


def _stream_block(*shape):
    return pl.BlockSpec((1,) + shape, lambda bi: (bi,) + (0,) * len(shape))


def _mla_attn(qn, qr, ckv_new, kr_new, ckv_cache, kr_cache, layer, w_k, w_v):
    b, sq, _ = qn.shape
    past = ckv_cache.shape[2]
    t_pad = -(-(past + sq) // LANES) * LANES
    cache = lambda width: pl.BlockSpec((1, 1, past, width), lambda bi: (layer, bi, 0, 0))
    return pl.pallas_call(
        functools.partial(_mla_attn_kernel, t_pad=t_pad),
        grid=(b,),
        in_specs=[_stream_block(sq, MLA_QN), _stream_block(sq, MLA_QR),
                  _stream_block(sq, MLA_KV_LORA), _stream_block(sq, MLA_ROPE),
                  cache(MLA_KV_LORA), cache(MLA_ROPE), _resident(w_k.shape), _resident(w_v.shape)],
        out_specs=_stream_block(sq, MLA_VW),
        out_shape=jax.ShapeDtypeStruct((b, sq, MLA_VW), BF16),
        compiler_params=_params("parallel"),
        name="mla_attn",
    )(qn, qr, ckv_new, kr_new, ckv_cache, kr_cache, w_k, w_v)


def _sb_proj_kernel(x_ref, g_ref, w_ref, wvt_ref, *refs, transposed):
    q_ref, k_ref, v_ref, kb_ref, vb_ref = refs[-5:]
    h = _rms(x_ref[...], g_ref[...]).astype(BF16)
    qkv = jnp.dot(h, w_ref[...], preferred_element_type=F32)
    q_ref[...] = (qkv[:, :SB_WIDTH] * (SB_HEAD_DIM ** -0.5 * LOG2E)).astype(BF16)
    k = qkv[:, SB_WIDTH:2 * SB_WIDTH]
    v = qkv[:, 2 * SB_WIDTH:]
    k_ref[0] = k
    v_ref[0] = v
    kb_ref[...] = k.astype(BF16)
    if transposed:
        vb_ref[0, 0] = lax.dot_general(wvt_ref[...], h, NT, preferred_element_type=F32).astype(BF16)
    else:
        vb_ref[...] = v.astype(BF16)


def _sb_proj(x, g, w_in, w_vt, layer, n_layers, kv_slabs, *, tm, blocks_per_batch=None):
    n = x.shape[0]
    transposed = blocks_per_batch is not None
    row = pl.BlockSpec((tm, SB_WIDTH), lambda i: (i, 0))
    slab = pl.BlockSpec((1, tm, SB_WIDTH), lambda i: (layer, i, 0))
    if transposed:
        nkb = blocks_per_batch
        vb_spec = pl.BlockSpec((1, 1, SB_WIDTH, tm), lambda i: (i // nkb, i % nkb, 0, 0))
        vb_shape = jax.ShapeDtypeStruct((n // (tm * nkb), nkb, SB_WIDTH, tm), BF16)
    else:
        vb_spec, vb_shape = row, jax.ShapeDtypeStruct((n, SB_WIDTH), BF16)
    in_specs = [pl.BlockSpec((tm, D_MODEL), lambda i: (i, 0)), _resident(g.shape),
                _resident(w_in.shape), _resident(w_vt.shape)]
    aliases = {}
    if kv_slabs is not None:
        in_specs += [pl.BlockSpec(memory_space=pl.ANY)] * 2
        aliases = {4: 1, 5: 2}
    return pl.pallas_call(
        functools.partial(_sb_proj_kernel, transposed=transposed),
        grid=(n // tm,),
        in_specs=in_specs,
        out_specs=[row, slab, slab, row, vb_spec],
        out_shape=[jax.ShapeDtypeStruct((n, SB_WIDTH), BF16),
                   jax.ShapeDtypeStruct((n_layers, n, SB_WIDTH), F32),
                   jax.ShapeDtypeStruct((n_layers, n, SB_WIDTH), F32),
                   jax.ShapeDtypeStruct((n, SB_WIDTH), BF16),
                   vb_shape],
        input_output_aliases=aliases,
        compiler_params=_params("parallel"),
        name="sb_proj_t" if transposed else "sb_proj",
    )(x, g, w_in, w_vt, *(kv_slabs or ()))


def _sb_attn_t_kernel(q_ref, k_ref, vt_ref, o_ref, *, tq, groups):
    tk = tq
    qi = pl.program_id(2)
    prev = jnp.maximum(qi - 1, 0)
    has_prev = (qi > 0).astype(F32)
    lane_head = lax.broadcasted_iota(jnp.int32, (tq, LANES), 1) // SB_HEAD_DIM
    group = lambda g: slice(g * LANES, (g + 1) * LANES)
    heads = [(g, hh) for g in range(groups) for hh in range(2)]
    q2 = [q_ref[0, :, group(g)] for g in range(groups)]
    qh = {h: jnp.where(lane_head == h[1], q2[h[0]], jnp.zeros_like(q2[h[0]])) for h in heads}
    key_row = lax.broadcasted_iota(jnp.int32, (tk, tk), 0)
    col = lax.broadcasted_iota(jnp.int32, (tk, tk), 1)
    later = (key_row < col).astype(BF16)
    causal = key_row < col

    def logits(kb, h):
        start = pl.multiple_of(kb * tk, tk)
        return lax.dot_general(k_ref[0, pl.ds(start, tk), group(h[0])], qh[h], NT,
                               preferred_element_type=F32)

    def suffix_in_block(sp):
        return jnp.dot(later, sp.astype(BF16), preferred_element_type=F32)

    def values(kb, h, a):
        return jnp.dot(vt_ref[0, kb, group(h[0]), :], a, preferred_element_type=F32)

    def col_sum(x):
        return jnp.sum(x, axis=0, keepdims=True)

    def window():
        z_diag = {h: logits(qi, h) for h in heads}
        z_prev = {h: logits(prev, h) for h in heads}
        sp_diag = {h: _softplus2(z_diag[h]) for h in heads}
        sp_prev = {h: _softplus2(z_prev[h]) for h in heads}
        spm = {h: jnp.where(causal, sp_diag[h], 0.0) for h in heads}
        suf_diag = {h: suffix_in_block(spm[h]) for h in heads}
        suf_prev = {h: suffix_in_block(sp_prev[h]) for h in heads}
        diag_sum = {h: col_sum(spm[h]) for h in heads}
        a_diag = {h: jnp.where(causal, jnp.exp2(z_diag[h] - sp_diag[h] - suf_diag[h]), 0.0).astype(BF16)
                  for h in heads}
        a_prev = {h: jnp.exp2(z_prev[h] - sp_prev[h] - suf_prev[h] - diag_sum[h]).astype(BF16)
                  for h in heads}
        pv_diag = {h: values(qi, h, a_diag[h]) for h in heads}
        pv_prev = {h: values(prev, h, a_prev[h]) for h in heads}
        carry = {h: diag_sum[h] + has_prev * col_sum(sp_prev[h]) for h in heads}
        return carry, {h: pv_diag[h] + has_prev * pv_prev[h] for h in heads}

    def earlier(kb, hs, carry, acc):
        zt = [logits(kb, h) for h in hs]
        sp = [_softplus2(z) for z in zt]
        suf = [suffix_in_block(s) for s in sp]
        a = [jnp.exp2(zt[i] - sp[i] - suf[i] - carry[i]).astype(BF16) for i in range(len(hs))]
        acc = [acc[i] + values(kb, hs[i], a[i]) for i in range(len(hs))]
        return [carry[i] + col_sum(sp[i]) for i in range(len(hs))], acc

    def unfinished(c):
        return jnp.minimum(jnp.min(c[0]), jnp.min(c[1])) < SB_LOG2_ZERO

    def cond(state):
        return jnp.logical_and(state[0] >= 0, state[1])

    carry, acc = window()
    row_head = lax.broadcasted_iota(jnp.int32, (LANES, tq), 0) // SB_HEAD_DIM
    for g in range(groups):
        hs = [(g, 0), (g, 1)]

        def body(state, hs=hs):
            kb, _, c, a = state
            c, a = earlier(kb, hs, c, a)
            return kb - 1, unfinished(c), c, a

        c0 = [carry[h] for h in hs]
        a0, a1 = lax.while_loop(cond, body, (prev - 1, unfinished(c0), c0, [acc[h] for h in hs]))[3]
        o_ref[0, :, group(g)] = jnp.where(row_head == 0, a0, a1).T.astype(o_ref.dtype)


def _sb_attn_t(q, k, vt, *, tq, groups=2):
    b, s, _ = q.shape
    nkb = vt.shape[1]
    assert s % tq == 0 and nkb * tq == s and vt.shape[3] == tq
    pairs = SB_WIDTH // LANES
    gw = groups * LANES
    return pl.pallas_call(
        functools.partial(_sb_attn_t_kernel, tq=tq, groups=groups),
        grid=(b, pairs // groups, s // tq),
        in_specs=[pl.BlockSpec((1, tq, gw), lambda bi, h, qi: (bi, qi, h)),
                  pl.BlockSpec((1, s, gw), lambda bi, h, qi: (bi, 0, h)),
                  pl.BlockSpec((1, nkb, gw, tq), lambda bi, h, qi: (bi, 0, h, 0))],
        out_specs=pl.BlockSpec((1, tq, gw), lambda bi, h, qi: (bi, qi, h)),
        out_shape=jax.ShapeDtypeStruct((b, s, SB_WIDTH), BF16),
        compiler_params=_params("parallel", "parallel", "arbitrary"),
        name="sb_attn_t",
    )(q, k, vt)


def _sb_attn_kernel(q_ref, kn_ref, vn_ref, kc_ref, vc_ref, o_ref, *, tk):
    sq = q_ref.shape[1]
    rows = SB_HEADS * sq
    n_cache = kc_ref.shape[2] // tk
    head = lambda h: slice(h * SB_HEAD_DIM, (h + 1) * SB_HEAD_DIM)
    q = [q_ref[0, :, head(h)] for h in range(SB_HEADS)]
    later = (lax.broadcasted_iota(jnp.int32, (tk, tk), 0)
             > lax.broadcasted_iota(jnp.int32, (tk, tk), 1)).astype(BF16)
    pad = jnp.zeros((tk - sq, SB_HEAD_DIM), BF16)

    def new_rows(ref, h):
        return jnp.concatenate([ref[0, :, head(h)], pad], axis=0)

    def cache_rows(ref, kb, h):
        return ref[0, 0, pl.ds(kb * tk, tk), h, :].astype(BF16)

    def block(keys, vals, carry, acc, masked):
        z = jnp.concatenate([lax.dot_general(q[h], keys(h), NT, preferred_element_type=F32)
                             for h in range(SB_HEADS)], axis=0)
        sp = _softplus2(z)
        spm = sp
        if masked:
            ok = (lax.broadcasted_iota(jnp.int32, (rows, tk), 1)
                  < lax.rem(lax.broadcasted_iota(jnp.int32, (rows, tk), 0), sq))
            spm = jnp.where(ok, sp, 0.0)
        suffix = jnp.dot(spm.astype(BF16), later, preferred_element_type=F32) + carry
        a = jnp.exp2(z - sp - suffix)
        if masked:
            a = jnp.where(ok, a, 0.0)
        a = a.astype(BF16)
        pv = jnp.concatenate([jnp.dot(a[sq * h:sq * (h + 1)], vals(h), preferred_element_type=F32)
                              for h in range(SB_HEADS)], axis=0)
        return carry + jnp.sum(spm, axis=1, keepdims=True), acc + pv

    carry, acc = block(lambda h: new_rows(kn_ref, h), lambda h: new_rows(vn_ref, h),
                       jnp.zeros((rows, 1), F32), jnp.zeros((rows, SB_HEAD_DIM), F32), True)

    def cond(state):
        return jnp.logical_and(state[0] >= 0, state[1])

    def body(state):
        kb, _, carry, acc = state
        carry, acc = block(lambda h: cache_rows(kc_ref, kb, h), lambda h: cache_rows(vc_ref, kb, h),
                           carry, acc, False)
        return kb - 1, jnp.min(carry) < SB_LOG2_ZERO, carry, acc

    acc = lax.while_loop(cond, body, (n_cache - 1, jnp.min(carry) < SB_LOG2_ZERO, carry, acc))[3]
    for h in range(SB_HEADS):
        o_ref[0, :, head(h)] = acc[sq * h:sq * (h + 1)].astype(o_ref.dtype)


def _sb_attn(q, k_new, v_new, k_cache, v_cache, layer, *, tk):
    b, sq, _ = q.shape
    n_layers, _, past, heads, hd = k_cache.shape
    assert past % tk == 0 and sq <= tk and heads == SB_HEADS and hd == SB_HEAD_DIM
    cache = pl.BlockSpec((1, 1, past, heads, hd), lambda bi: (layer, bi, 0, 0, 0))
    return pl.pallas_call(
        functools.partial(_sb_attn_kernel, tk=tk),
        grid=(b,),
        in_specs=[_stream_block(sq, SB_WIDTH), _stream_block(sq, SB_WIDTH), _stream_block(sq, SB_WIDTH),
                  cache, cache],
        out_specs=_stream_block(sq, SB_WIDTH),
        out_shape=jax.ShapeDtypeStruct((b, sq, SB_WIDTH), BF16),
        compiler_params=_params("parallel"),
        name="sb_attn",
    )(q, k_new, v_new, k_cache, v_cache)


def _post_kernel(x_ref, o_ref, wo_ref, g_ref, wup_ref, wdn_ref, gfin_ref, y_ref, *, final, ff_blk):
    x1 = x_ref[...] + jnp.dot(o_ref[...], wo_ref[...], preferred_element_type=F32)
    h = _rms(x1, g_ref[...]).astype(BF16)
    acc = x1
    for c in range(0, D_FF, ff_blk):
        up = jnp.dot(h, wup_ref[:, c:c + ff_blk], preferred_element_type=F32)
        act = jnp.square(jnp.maximum(up, 0.0)).astype(BF16)
        acc = acc + jnp.dot(act, wdn_ref[c:c + ff_blk, :], preferred_element_type=F32)
    y_ref[...] = _rms(acc, gfin_ref[...]) if final else acc


def _post(x, o, w_o, g_mlp, w_up, w_dn, g_fin, *, final):
    n = x.shape[0]
    tm = _row_block(n)
    row = pl.BlockSpec((tm, D_MODEL), lambda i: (i, 0))
    kern = functools.partial(_post_kernel, final=final, ff_blk=1024)
    return pl.pallas_call(
        kern,
        grid=(n // tm,),
        in_specs=[row, row, _resident(w_o.shape), _resident(g_mlp.shape), _resident(w_up.shape),
                  _resident(w_dn.shape), _resident(g_fin.shape)],
        out_specs=row,
        out_shape=jax.ShapeDtypeStruct((n, D_MODEL), F32),
        compiler_params=_params("parallel"),
        name="post_final" if final else "post",
    )(x, o, w_o, g_mlp, w_up, w_dn, g_fin)


def _cast_kernel(x_ref, o_ref):
    o_ref[...] = x_ref[...].astype(o_ref.dtype)


def _cast_bf16(w):
    cols = w.shape[-1]
    rows = w.size // cols
    tm = rows
    while tm * cols * 4 > CAST_BLOCK_BYTES and tm % 32 == 0:
        tm //= 2
    spec = pl.BlockSpec((tm, cols), lambda i: (i, 0))
    out = pl.pallas_call(
        _cast_kernel, grid=(rows // tm,), in_specs=[spec], out_specs=spec,
        out_shape=jax.ShapeDtypeStruct((rows, cols), BF16),
        compiler_params=_params("parallel"), name="cast_bf16",
    )(w.reshape(rows, cols))
    return out.reshape(w.shape)


def _prep_weights(norm_mix, norm_mlp, norm_final, mla_w_in, mla_g_q, mla_w_qb, mla_g_kv,
                  mla_w_kvb, mla_w_o, sb_w_in, sb_w_o, mlp_w_up, mlp_w_down):
    half = MLA_ROPE // 2
    lat = MLA_Q_LORA + MLA_KV_LORA
    w_in_ext = jnp.concatenate(
        [mla_w_in, -mla_w_in[:, :, lat + half:], mla_w_in[:, :, lat:lat + half]], axis=2)
    n_mla = mla_w_qb.shape[0]
    qb = mla_w_qb.reshape(n_mla, MLA_Q_LORA, MLA_HEADS, MLA_NOPE + MLA_ROPE)
    qb_rot = jnp.concatenate([-qb[..., MLA_NOPE + half:], qb[..., MLA_NOPE:MLA_NOPE + half]], axis=-1)
    w_qb_ext = jnp.concatenate(
        [qb[..., :MLA_NOPE].reshape(n_mla, MLA_Q_LORA, MLA_QN),
         qb[..., MLA_NOPE:].reshape(n_mla, MLA_Q_LORA, MLA_QR),
         qb_rot.reshape(n_mla, MLA_Q_LORA, MLA_QR)], axis=2)
    kvb = mla_w_kvb.reshape(n_mla, MLA_KV_LORA, MLA_HEADS, MLA_NOPE + MLA_V)
    w_k = kvb[..., :MLA_NOPE].reshape(n_mla, MLA_KV_LORA, MLA_QN).astype(BF16)
    w_v = kvb[..., MLA_NOPE:].reshape(n_mla, MLA_KV_LORA, MLA_VW).astype(BF16)
    sb_w_in = _cast_bf16(sb_w_in)
    return dict(
        norm_mix=norm_mix[:, None, :], norm_mlp=norm_mlp[:, None, :], norm_final=norm_final[None, :],
        w_in_ext=w_in_ext.astype(BF16), g_q=mla_g_q[:, None, :], g_kv=mla_g_kv[:, None, :],
        w_qb_ext=w_qb_ext.astype(BF16), w_k=w_k, w_v=w_v, w_vt=jnp.swapaxes(w_v, 1, 2),
        mla_w_o=_cast_bf16(mla_w_o), sb_w_in=sb_w_in,
        sb_w_vt=jnp.swapaxes(sb_w_in[:, :, 2 * SB_WIDTH:], 1, 2), sb_w_o=_cast_bf16(sb_w_o),
        w_up=_cast_bf16(mlp_w_up), w_dn=_cast_bf16(mlp_w_down))


def _rope_tables(pos, reps):
    inv = ROPE_THETA ** (-jnp.arange(0, MLA_ROPE, 2, dtype=F32) / MLA_ROPE)
    ang = pos.astype(F32)[:, None] * inv[None, :]
    tile = lambda t: jnp.tile(t, (reps, LANES // (MLA_ROPE // 2)))
    return tile(jnp.cos(ang)), tile(jnp.sin(ang))


def _trunk(x, caches, w):
    b, s, _ = x.shape
    n = b * s
    prompt = caches is None
    past = 0 if prompt else caches[0].shape[2]
    if prompt:
        mla_t = _row_block(s, MLA_TILE)
        sb_t = _row_block(s // 2, SB_TILE)
    pos = past + jnp.arange(s, dtype=jnp.int32)
    cos_t, sin_t = _rope_tables(pos, 1 if prompt else b)
    xf = x.reshape(n, D_MODEL)
    ckv_l, kr_l, kv_slabs = [], [], None
    for i in range(DEPTH):
        j = i // 2
        if i % 2 == 0:
            ckv, krope, kr2, qn, qr = _mla_proj(xf, w["norm_mix"][i], w["w_in_ext"][j], w["g_q"][j],
                                                w["g_kv"][j], w["w_qb_ext"][j], cos_t, sin_t)
            ckv_l.append(ckv.reshape(b, s, MLA_KV_LORA))
            krope = krope.reshape(b, s, MLA_ROPE)
            kr_l.append(krope)
            qn = qn.reshape(b, s, MLA_QN)
            qr = qr.reshape(b, s, MLA_QR)
            if prompt:
                kn, vt = _mla_kv(ckv, w["w_k"][j], w["w_vt"][j], tm=mla_t, blocks_per_batch=s // mla_t)
                o = _mla_attn_t(qn, qr, kn.reshape(b, s, MLA_QN), kr2.reshape(2, b, s, LANES), vt,
                                tq=mla_t, tk=mla_t)
            else:
                o = _mla_attn(qn, qr, ckv_l[-1], krope, caches[0], caches[1], j, w["w_k"][j], w["w_v"][j])
            w_o = w["mla_w_o"][j]
        else:
            proj = functools.partial(_sb_proj, xf, w["norm_mix"][i], w["sb_w_in"][j], w["sb_w_vt"][j],
                                     j, DEPTH // 2, kv_slabs)
            if prompt:
                q, k, v, kb, vb = proj(tm=sb_t, blocks_per_batch=s // sb_t)
                o = _sb_attn_t(q.reshape(b, s, SB_WIDTH), kb.reshape(b, s, SB_WIDTH), vb, tq=sb_t)
            else:
                q, k, v, kb, vb = proj(tm=_row_block(n))
                rows = lambda t: t.reshape(b, s, SB_WIDTH)
                o = _sb_attn(rows(q), rows(kb), rows(vb), caches[2], caches[3], j, tk=LANES)
            kv_slabs = (k, v)
            w_o = w["sb_w_o"][j]
        xf = _post(xf, o.reshape(n, D_MODEL), w_o, w["norm_mlp"][i], w["w_up"][i], w["w_dn"][i],
                   w["norm_final"], final=(i == DEPTH - 1))
    heads = lambda t: t.reshape(DEPTH // 2, b, s, SB_HEADS, SB_HEAD_DIM)
    return (xf.reshape(b, s, D_MODEL), jnp.stack(ckv_l), jnp.stack(kr_l), heads(kv_slabs[0]),
            heads(kv_slabs[1]))


def kernel(x_prompt, x_sample, cache_mla_ckv, cache_mla_krope, cache_sb_k, cache_sb_v, norm_mix, norm_mlp, norm_final, mla_w_in, mla_g_q, mla_w_qb, mla_g_kv, mla_w_kvb, mla_w_o, sb_w_in, sb_w_o, mlp_w_up, mlp_w_down):
    w = _prep_weights(norm_mix, norm_mlp, norm_final, mla_w_in, mla_g_q, mla_w_qb, mla_g_kv,
                      mla_w_kvb, mla_w_o, sb_w_in, sb_w_o, mlp_w_up, mlp_w_down)
    y_p, p_ckv, p_kr, p_k, p_v = _trunk(x_prompt, None, w)
    y_s, s_ckv, s_kr, s_k, s_v = _trunk(
        x_sample, (cache_mla_ckv, cache_mla_krope, cache_sb_k, cache_sb_v), w)
    return (y_p, y_s, p_ckv, p_kr, p_k, p_v, s_ckv, s_kr, s_k, s_v)
```

```python
import functools
import math

import jax
import jax.numpy as jnp
from jax import lax
from jax.experimental import pallas as pl
from jax.experimental.pallas import tpu as pltpu

D_MODEL = 1024
DEPTH = 4
CHUNK = 64
MLA_HEADS = 8
MLA_NOPE = 128
MLA_ROPE = 64
MLA_V = 128
MLA_Q_LORA = 512
MLA_KV_LORA = 256
ROPE_THETA = 10000.0
SB_HEADS = 16
SB_HEAD_DIM = 64
SB_WIDTH = SB_HEADS * SB_HEAD_DIM
D_FF = 4 * D_MODEL
EPS = 1e-6

LANES = 128
MLA_IN_EXT = MLA_Q_LORA + MLA_KV_LORA + 2 * MLA_ROPE
MLA_QN = MLA_HEADS * MLA_NOPE
MLA_QR = MLA_HEADS * MLA_ROPE
MLA_VW = MLA_HEADS * MLA_V
VMEM_LIMIT_BYTES = 56 * 1024 * 1024
MLA_TILE = 512
SB_TILE = 256
CAST_BLOCK_BYTES = 4 * 1024 * 1024
LOG2E = math.log2(math.e)
LN2 = math.log(2.0)

SB_LOG2_ZERO = 151.0

F32 = jnp.float32
BF16 = jnp.bfloat16
NT = (((1,), (1,)), ((), ()))


def _rms(x, g):
    return x * lax.rsqrt(jnp.mean(x * x, axis=-1, keepdims=True) + EPS) * g


def _params(*sem):
    return pltpu.CompilerParams(dimension_semantics=sem, vmem_limit_bytes=VMEM_LIMIT_BYTES)


def _resident(shape):
    nd = len(shape)
    return pl.BlockSpec(shape, lambda *_: (0,) * nd, pipeline_mode=pl.Buffered(1))


def _row_block(n, cap=512):
    for tm in (512, 256, 128, 64, 32, 16, 8):
        if tm <= cap and n % tm == 0:
            return tm
    raise ValueError(f"row count {n} is not a multiple of 8")


def _softplus2(z2):
    return jnp.maximum(z2, 0.0) + jnp.log(1.0 + jnp.exp2(-jnp.abs(z2))) * (1.0 / LN2)


def _mla_proj_kernel(x_ref, gmix_ref, win_ref, gq_ref, gkv_ref, wqb_ref, cos_ref, sin_ref,
                     ckv_ref, krope_ref, kr2_ref, qn_ref, qr_ref):
    scale = (MLA_NOPE + MLA_ROPE) ** -0.5 * LOG2E
    h = _rms(x_ref[...], gmix_ref[...]).astype(BF16)
    a = jnp.dot(h, win_ref[...], preferred_element_type=F32)
    c_q = _rms(a[:, :MLA_Q_LORA], gq_ref[...])
    c_kv = _rms(a[:, MLA_Q_LORA:MLA_Q_LORA + MLA_KV_LORA], gkv_ref[...])
    ckv_ref[...] = c_kv
    cos = cos_ref[...]
    sin = sin_ref[...]
    kr_pair = a[:, MLA_Q_LORA + MLA_KV_LORA:]
    lane = lax.broadcasted_iota(jnp.int32, kr_pair.shape, 1)
    u = kr_pair * jnp.where(lane < MLA_ROPE, cos, sin)
    both = u + pltpu.roll(u, MLA_ROPE, axis=1)
    krope_ref[...] = both[:, :MLA_ROPE]
    kr2_ref[0] = jnp.where(lane < MLA_ROPE, both, 0.0).astype(BF16)
    kr2_ref[1] = jnp.where(lane < MLA_ROPE, 0.0, both).astype(BF16)
    q = jnp.dot(c_q.astype(BF16), wqb_ref[...], preferred_element_type=F32)
    qn_ref[...] = (q[:, :MLA_QN] * scale).astype(BF16)
    cos4 = jnp.concatenate([cos] * (MLA_QR // LANES), axis=1)
    sin4 = jnp.concatenate([sin] * (MLA_QR // LANES), axis=1)
    qr = q[:, MLA_QN:MLA_QN + MLA_QR] * cos4 + q[:, MLA_QN + MLA_QR:] * sin4
    qr_ref[...] = (qr * scale).astype(BF16)


def _mla_proj(x, g_mix, w_in_ext, g_q, g_kv, w_qb_ext, cos_t, sin_t):
    n = x.shape[0]
    tm = min(_row_block(n), _row_block(cos_t.shape[0]))
    n_tab = cos_t.shape[0] // tm
    row = lambda w: pl.BlockSpec((tm, w), lambda i: (i, 0))
    tab = pl.BlockSpec((tm, LANES), lambda i: (i % n_tab, 0))
    return pl.pallas_call(
        _mla_proj_kernel,
        grid=(n // tm,),
        in_specs=[row(D_MODEL), _resident(g_mix.shape), _resident(w_in_ext.shape),
                  _resident(g_q.shape), _resident(g_kv.shape), _resident(w_qb_ext.shape),
                  tab, tab],
        out_specs=[row(MLA_KV_LORA), row(MLA_ROPE), pl.BlockSpec((2, tm, LANES), lambda i: (0, i, 0)),
                   row(MLA_QN), row(MLA_QR)],
        out_shape=[jax.ShapeDtypeStruct((n, MLA_KV_LORA), F32),
                   jax.ShapeDtypeStruct((n, MLA_ROPE), F32),
                   jax.ShapeDtypeStruct((2, n, LANES), BF16),
                   jax.ShapeDtypeStruct((n, MLA_QN), BF16),
                   jax.ShapeDtypeStruct((n, MLA_QR), BF16)],
        compiler_params=_params("parallel"),
        name="mla_proj",
    )(x, g_mix, w_in_ext, g_q, g_kv, w_qb_ext, cos_t, sin_t)


def _mla_kv_kernel(ckv_ref, wk_ref, wvt_ref, kn_ref, vt_ref):
    c = ckv_ref[...].astype(BF16)
    kn_ref[...] = jnp.dot(c, wk_ref[...], preferred_element_type=F32).astype(BF16)
    vt_ref[0, 0] = lax.dot_general(wvt_ref[...], c, NT, preferred_element_type=F32).astype(BF16)


def _mla_kv(ckv, w_k, w_vt, *, tm, blocks_per_batch):
    n = ckv.shape[0]
    nkb = blocks_per_batch
    row = lambda w: pl.BlockSpec((tm, w), lambda i: (i, 0))
    return pl.pallas_call(
        _mla_kv_kernel,
        grid=(n // tm,),
        in_specs=[row(MLA_KV_LORA), _resident(w_k.shape), _resident(w_vt.shape)],
        out_specs=[row(MLA_QN), pl.BlockSpec((1, 1, MLA_VW, tm), lambda i: (i // nkb, i % nkb, 0, 0))],
        out_shape=[jax.ShapeDtypeStruct((n, MLA_QN), BF16),
                   jax.ShapeDtypeStruct((n // (tm * nkb), nkb, MLA_VW, tm), BF16)],
        compiler_params=_params("parallel"),
        name="mla_kv_t",
    )(ckv, w_k, w_vt)


def _mla_attn_t_kernel(qn_ref, qr_ref, kn_ref, kr_ref, vt_ref, o_ref, acc_ref, *, tq, tk):
    qi = pl.program_id(2)
    p0 = qi * tq
    qr = qr_ref[0]
    qs = [jnp.concatenate([qn_ref[0, :, hh * LANES:(hh + 1) * LANES], qr], axis=1)
          for hh in range(2)]
    acc_ref[...] = jnp.zeros_like(acc_ref)

    def scores(kb, hh):
        start = pl.multiple_of(kb * tk, tk)
        k = jnp.concatenate([kn_ref[0, pl.ds(start, tk), hh * LANES:(hh + 1) * LANES],
                             kr_ref[hh, 0, pl.ds(start, tk), :]], axis=1)
        return lax.dot_general(k, qs[hh], NT, preferred_element_type=F32)

    def consume(kb, hh, st, m, l):
        m_new = jnp.maximum(m, jnp.max(st, axis=0, keepdims=True))
        alpha = jnp.exp2(m - m_new)
        pt = jnp.exp2(st - m_new)
        l = alpha * l + jnp.sum(pt, axis=0, keepdims=True)
        pv = jnp.dot(vt_ref[0, kb, hh * MLA_V:(hh + 1) * MLA_V, :], pt.astype(BF16),
                     preferred_element_type=F32)
        acc_ref[hh] = alpha * acc_ref[hh] + pv
        return m_new, l

    k_pos = lax.broadcasted_iota(jnp.int32, (tk, tq), 0)
    q_pos = lax.broadcasted_iota(jnp.int32, (tk, tq), 1)
    visible = k_pos // CHUNK <= q_pos // CHUNK

    def run(base, n, carry, mask_last=False):
        st = [scores(base, hh) for hh in range(2)]
        for i in range(n):
            nxt = [scores(base + i + 1, hh) for hh in range(2)] if i + 1 < n else None
            if mask_last and i == n - 1:
                st = [jnp.where(visible, s, -jnp.inf) for s in st]
            carry = tuple(consume(base + i, hh, st[hh], *carry[hh]) for hh in range(2))
            st = nxt
        return carry

    n_full = p0 // tk
    n_loop = jnp.maximum(n_full - 1, 0)
    init = (jnp.full((1, tq), -1e30, F32), jnp.zeros((1, tq), F32))
    carry = lax.fori_loop(0, n_loop // 4, lambda j, c: run(4 * j, 4, c), (init, init))
    carry = lax.fori_loop(2 * (n_loop // 4), n_loop // 2, lambda j, c: run(2 * j, 2, c), carry)
    carry = lax.fori_loop(2 * (n_loop // 2), n_loop, lambda j, c: run(j, 1, c), carry)
    carry = lax.cond(n_full > 0,
                     lambda c: run(n_full - 1, 2, c, mask_last=True),
                     lambda c: run(n_full, 1, c, mask_last=True), carry)
    for hh in range(2):
        o = acc_ref[hh] * (1.0 / carry[hh][1])
        o_ref[0, :, hh * MLA_V:(hh + 1) * MLA_V] = o.T.astype(o_ref.dtype)


def _mla_attn_t(qn, qr, kn, kr2, vt, *, tq, tk):
    b, s, _ = qn.shape
    nkb = vt.shape[1]
    assert tq == tk and s % tq == 0 and kn.shape[1] == s and nkb * tk == s
    kern = functools.partial(_mla_attn_t_kernel, tq=tq, tk=tk)
    return pl.pallas_call(
        kern,
        grid=(b, MLA_HEADS // 2, s // tq),
        in_specs=[pl.BlockSpec((1, tq, 2 * LANES), lambda bi, hp, qi: (bi, qi, hp)),
                  pl.BlockSpec((1, tq, LANES), lambda bi, hp, qi: (bi, qi, hp)),
                  pl.BlockSpec((1, s, 2 * LANES), lambda bi, hp, qi: (bi, 0, hp)),
                  pl.BlockSpec((2, 1, s, LANES), lambda bi, hp, qi: (0, bi, 0, 0)),
                  pl.BlockSpec((1, nkb, 2 * MLA_V, tk), lambda bi, hp, qi: (bi, 0, hp, 0))],
        out_specs=pl.BlockSpec((1, tq, 2 * MLA_V), lambda bi, hp, qi: (bi, qi, hp)),
        out_shape=jax.ShapeDtypeStruct((b, s, MLA_VW), BF16),
        scratch_shapes=[pltpu.VMEM((2, MLA_V, tq), F32)],
        compiler_params=_params("parallel", "parallel", "arbitrary"),
        name="mla_attn_t",
    )(qn, qr, kn, kr2, vt)


def _mla_attn_kernel(qn_ref, qr_ref, cn_ref, rn_ref, cc_ref, rc_ref, wk_ref, wv_ref, o_ref, *, t_pad):
    sq = qn_ref.shape[1]
    past = cc_ref.shape[2]
    t_valid = past + sq

    def all_rows(cache, new):
        width = new.shape[1]
        return jnp.concatenate([cache.astype(BF16), new.astype(BF16),
                                jnp.zeros((t_pad - t_valid, width), BF16)], axis=0)

    latent = all_rows(cc_ref[0, 0], cn_ref[0])
    rope = all_rows(rc_ref[0, 0], rn_ref[0])
    r_in = lax.broadcasted_iota(jnp.int32, (MLA_ROPE, LANES), 0)
    r_out = lax.broadcasted_iota(jnp.int32, (MLA_ROPE, LANES), 1)
    rope_half = [jnp.dot(rope, (r_out == r_in + hh * MLA_ROPE).astype(BF16),
                         preferred_element_type=F32).astype(BF16) for hh in range(2)]
    q_pos = past + lax.broadcasted_iota(jnp.int32, (sq, t_pad), 0)
    k_pos = lax.broadcasted_iota(jnp.int32, (sq, t_pad), 1)
    visible = (k_pos // CHUNK <= q_pos // CHUNK) & (k_pos < t_valid)
    for pair in range(MLA_HEADS // 2):
        two = slice(pair * 2 * LANES, (pair + 1) * 2 * LANES)
        kn = jnp.dot(latent, wk_ref[:, two], preferred_element_type=F32).astype(BF16)
        v = jnp.dot(latent, wv_ref[:, two], preferred_element_type=F32).astype(BF16)
        qr = qr_ref[0, :, pair * LANES:(pair + 1) * LANES]
        for hh in range(2):
            cols = slice((2 * pair + hh) * LANES, (2 * pair + hh + 1) * LANES)
            half = slice(hh * LANES, (hh + 1) * LANES)
            q = jnp.concatenate([qn_ref[0, :, cols], qr], axis=1)
            k = jnp.concatenate([kn[:, half], rope_half[hh]], axis=1)
            s = jnp.where(visible, lax.dot_general(q, k, NT, preferred_element_type=F32), -jnp.inf)
            p = jnp.exp2(s - jnp.max(s, axis=1, keepdims=True))
            o = jnp.dot(p.astype(BF16), v[:, half], preferred_element_type=F32)
            o_ref[0, :, cols] = (o / jnp.sum(p, axis=1, keepdims=True)).astype(o_ref.dtype)


def _stream_block(*shape):
    return pl.BlockSpec((1,) + shape, lambda bi: (bi,) + (0,) * len(shape))


def _mla_attn(qn, qr, ckv_new, kr_new, ckv_cache, kr_cache, layer, w_k, w_v):
    b, sq, _ = qn.shape
    past = ckv_cache.shape[2]
    t_pad = -(-(past + sq) // LANES) * LANES
    cache = lambda width: pl.BlockSpec((1, 1, past, width), lambda bi: (layer, bi, 0, 0))
    return pl.pallas_call(
        functools.partial(_mla_attn_kernel, t_pad=t_pad),
        grid=(b,),
        in_specs=[_stream_block(sq, MLA_QN), _stream_block(sq, MLA_QR),
                  _stream_block(sq, MLA_KV_LORA), _stream_block(sq, MLA_ROPE),
                  cache(MLA_KV_LORA), cache(MLA_ROPE), _resident(w_k.shape), _resident(w_v.shape)],
        out_specs=_stream_block(sq, MLA_VW),
        out_shape=jax.ShapeDtypeStruct((b, sq, MLA_VW), BF16),
        compiler_params=_params("parallel"),
        name="mla_attn",
    )(qn, qr, ckv_new, kr_new, ckv_cache, kr_cache, w_k, w_v)


def _sb_proj_kernel(x_ref, g_ref, w_ref, wvt_ref, *refs, transposed):
    q_ref, k_ref, v_ref, kb_ref, vb_ref = refs[-5:]
    h = _rms(x_ref[...], g_ref[...]).astype(BF16)
    qkv = jnp.dot(h, w_ref[...], preferred_element_type=F32)
    q_ref[...] = (qkv[:, :SB_WIDTH] * (SB_HEAD_DIM ** -0.5 * LOG2E)).astype(BF16)
    k = qkv[:, SB_WIDTH:2 * SB_WIDTH]
    v = qkv[:, 2 * SB_WIDTH:]
    k_ref[0] = k
    v_ref[0] = v
    kb_ref[...] = k.astype(BF16)
    if transposed:
        vb_ref[0, 0] = lax.dot_general(wvt_ref[...], h, NT, preferred_element_type=F32).astype(BF16)
    else:
        vb_ref[...] = v.astype(BF16)


def _sb_proj(x, g, w_in, w_vt, layer, n_layers, kv_slabs, *, tm, blocks_per_batch=None):
    n = x.shape[0]
    transposed = blocks_per_batch is not None
    row = pl.BlockSpec((tm, SB_WIDTH), lambda i: (i, 0))
    slab = pl.BlockSpec((1, tm, SB_WIDTH), lambda i: (layer, i, 0))
    if transposed:
        nkb = blocks_per_batch
        vb_spec = pl.BlockSpec((1, 1, SB_WIDTH, tm), lambda i: (i // nkb, i % nkb, 0, 0))
        vb_shape = jax.ShapeDtypeStruct((n // (tm * nkb), nkb, SB_WIDTH, tm), BF16)
    else:
        vb_spec, vb_shape = row, jax.ShapeDtypeStruct((n, SB_WIDTH), BF16)
    in_specs = [pl.BlockSpec((tm, D_MODEL), lambda i: (i, 0)), _resident(g.shape),
                _resident(w_in.shape), _resident(w_vt.shape)]
    aliases = {}
    if kv_slabs is not None:
        in_specs += [pl.BlockSpec(memory_space=pl.ANY)] * 2
        aliases = {4: 1, 5: 2}
    return pl.pallas_call(
        functools.partial(_sb_proj_kernel, transposed=transposed),
        grid=(n // tm,),
        in_specs=in_specs,
        out_specs=[row, slab, slab, row, vb_spec],
        out_shape=[jax.ShapeDtypeStruct((n, SB_WIDTH), BF16),
                   jax.ShapeDtypeStruct((n_layers, n, SB_WIDTH), F32),
                   jax.ShapeDtypeStruct((n_layers, n, SB_WIDTH), F32),
                   jax.ShapeDtypeStruct((n, SB_WIDTH), BF16),
                   vb_shape],
        input_output_aliases=aliases,
        compiler_params=_params("parallel"),
        name="sb_proj_t" if transposed else "sb_proj",
    )(x, g, w_in, w_vt, *(kv_slabs or ()))


def _sb_attn_t_kernel(q_ref, k_ref, vt_ref, o_ref, *, tq, groups):
    tk = tq
    qi = pl.program_id(2)
    prev = jnp.maximum(qi - 1, 0)
    has_prev = (qi > 0).astype(F32)
    lane_head = lax.broadcasted_iota(jnp.int32, (tq, LANES), 1) // SB_HEAD_DIM
    group = lambda g: slice(g * LANES, (g + 1) * LANES)
    heads = [(g, hh) for g in range(groups) for hh in range(2)]
    q2 = [q_ref[0, :, group(g)] for g in range(groups)]
    qh = {h: jnp.where(lane_head == h[1], q2[h[0]], jnp.zeros_like(q2[h[0]])) for h in heads}
    key_row = lax.broadcasted_iota(jnp.int32, (tk, tk), 0)
    col = lax.broadcasted_iota(jnp.int32, (tk, tk), 1)
    later = (key_row < col).astype(BF16)
    causal = key_row < col

    def logits(kb, h):
        start = pl.multiple_of(kb * tk, tk)
        return lax.dot_general(k_ref[0, pl.ds(start, tk), group(h[0])], qh[h], NT,
                               preferred_element_type=F32)

    def suffix_in_block(sp):
        return jnp.dot(later, sp.astype(BF16), preferred_element_type=F32)

    def values(kb, h, a):
        return jnp.dot(vt_ref[0, kb, group(h[0]), :], a, preferred_element_type=F32)

    def col_sum(x):
        return jnp.sum(x, axis=0, keepdims=True)

    def window():
        z_diag = {h: logits(qi, h) for h in heads}
        z_prev = {h: logits(prev, h) for h in heads}
        sp_diag = {h: _softplus2(z_diag[h]) for h in heads}
        sp_prev = {h: _softplus2(z_prev[h]) for h in heads}
        spm = {h: jnp.where(causal, sp_diag[h], 0.0) for h in heads}
        suf_diag = {h: suffix_in_block(spm[h]) for h in heads}
        suf_prev = {h: suffix_in_block(sp_prev[h]) for h in heads}
        diag_sum = {h: col_sum(spm[h]) for h in heads}
        a_diag = {h: jnp.where(causal, jnp.exp2(z_diag[h] - sp_diag[h] - suf_diag[h]), 0.0).astype(BF16)
                  for h in heads}
        a_prev = {h: jnp.exp2(z_prev[h] - sp_prev[h] - suf_prev[h] - diag_sum[h]).astype(BF16)
                  for h in heads}
        pv_diag = {h: values(qi, h, a_diag[h]) for h in heads}
        pv_prev = {h: values(prev, h, a_prev[h]) for h in heads}
        carry = {h: diag_sum[h] + has_prev * col_sum(sp_prev[h]) for h in heads}
        return carry, {h: pv_diag[h] + has_prev * pv_prev[h] for h in heads}

    def earlier(kb, hs, carry, acc):
        zt = [logits(kb, h) for h in hs]
        sp = [_softplus2(z) for z in zt]
        suf = [suffix_in_block(s) for s in sp]
        a = [jnp.exp2(zt[i] - sp[i] - suf[i] - carry[i]).astype(BF16) for i in range(len(hs))]
        acc = [acc[i] + values(kb, hs[i], a[i]) for i in range(len(hs))]
        return [carry[i] + col_sum(sp[i]) for i in range(len(hs))], acc

    def unfinished(c):
        return jnp.minimum(jnp.min(c[0]), jnp.min(c[1])) < SB_LOG2_ZERO

    def cond(state):
        return jnp.logical_and(state[0] >= 0, state[1])

    carry, acc = window()
    row_head = lax.broadcasted_iota(jnp.int32, (LANES, tq), 0) // SB_HEAD_DIM
    for g in range(groups):
        hs = [(g, 0), (g, 1)]

        def body(state, hs=hs):
            kb, _, c, a = state
            c, a = earlier(kb, hs, c, a)
            return kb - 1, unfinished(c), c, a

        c0 = [carry[h] for h in hs]
        a0, a1 = lax.while_loop(cond, body, (prev - 1, unfinished(c0), c0, [acc[h] for h in hs]))[3]
        o_ref[0, :, group(g)] = jnp.where(row_head == 0, a0, a1).T.astype(o_ref.dtype)


def _sb_attn_t(q, k, vt, *, tq, groups=2):
    b, s, _ = q.shape
    nkb = vt.shape[1]
    assert s % tq == 0 and nkb * tq == s and vt.shape[3] == tq
    pairs = SB_WIDTH // LANES
    gw = groups * LANES
    return pl.pallas_call(
        functools.partial(_sb_attn_t_kernel, tq=tq, groups=groups),
        grid=(b, pairs // groups, s // tq),
        in_specs=[pl.BlockSpec((1, tq, gw), lambda bi, h, qi: (bi, qi, h)),
                  pl.BlockSpec((1, s, gw), lambda bi, h, qi: (bi, 0, h)),
                  pl.BlockSpec((1, nkb, gw, tq), lambda bi, h, qi: (bi, 0, h, 0))],
        out_specs=pl.BlockSpec((1, tq, gw), lambda bi, h, qi: (bi, qi, h)),
        out_shape=jax.ShapeDtypeStruct((b, s, SB_WIDTH), BF16),
        compiler_params=_params("parallel", "parallel", "arbitrary"),
        name="sb_attn_t",
    )(q, k, vt)


def _sb_attn_kernel(q_ref, kn_ref, vn_ref, kc_ref, vc_ref, o_ref, *, tk):
    sq = q_ref.shape[1]
    rows = SB_HEADS * sq
    n_cache = kc_ref.shape[4] // tk
    head = lambda h: slice(h * SB_HEAD_DIM, (h + 1) * SB_HEAD_DIM)
    q = [q_ref[0, :, head(h)] for h in range(SB_HEADS)]
    later = (lax.broadcasted_iota(jnp.int32, (tk, tk), 0)
             > lax.broadcasted_iota(jnp.int32, (tk, tk), 1)).astype(BF16)
    pad = jnp.zeros((tk - sq, SB_HEAD_DIM), BF16)

    def new_rows(ref, h):
        return jnp.concatenate([ref[0, :, head(h)], pad], axis=0)

    def cache_cols(ref, kb, h):
        return ref[0, 0, h, :, pl.ds(pl.multiple_of(kb * tk, tk), tk)].astype(BF16)

    def block(logits, values, carry, acc, masked):
        z = jnp.concatenate([logits(h) for h in range(SB_HEADS)], axis=0)
        sp = _softplus2(z)
        spm = sp
        if masked:
            ok = (lax.broadcasted_iota(jnp.int32, (rows, tk), 1)
                  < lax.rem(lax.broadcasted_iota(jnp.int32, (rows, tk), 0), sq))
            spm = jnp.where(ok, sp, 0.0)
        suffix = jnp.dot(spm.astype(BF16), later, preferred_element_type=F32) + carry
        a = jnp.exp2(z - sp - suffix)
        if masked:
            a = jnp.where(ok, a, 0.0)
        a = a.astype(BF16)
        pv = jnp.concatenate([values(h, a[sq * h:sq * (h + 1)]) for h in range(SB_HEADS)],
                             axis=0)
        return carry + jnp.sum(spm, axis=1, keepdims=True), acc + pv

    carry, acc = block(
        lambda h: lax.dot_general(q[h], new_rows(kn_ref, h), NT, preferred_element_type=F32),
        lambda h, a: jnp.dot(a, new_rows(vn_ref, h), preferred_element_type=F32),
        jnp.zeros((rows, 1), F32), jnp.zeros((rows, SB_HEAD_DIM), F32), True)

    def cond(state):
        return jnp.logical_and(state[0] >= 0, state[1])

    def body(state):
        kb, _, carry, acc = state
        carry, acc = block(
            lambda h: jnp.dot(q[h], cache_cols(kc_ref, kb, h), preferred_element_type=F32),
            lambda h, a: lax.dot_general(a, cache_cols(vc_ref, kb, h), NT, preferred_element_type=F32),
            carry, acc, False)
        return kb - 1, jnp.min(carry) < SB_LOG2_ZERO, carry, acc

    acc = lax.while_loop(cond, body, (n_cache - 1, jnp.min(carry) < SB_LOG2_ZERO, carry, acc))[3]
    for h in range(SB_HEADS):
        o_ref[0, :, head(h)] = acc[sq * h:sq * (h + 1)].astype(o_ref.dtype)


def _sb_attn(q, k_new, v_new, k_cache, v_cache, layer, *, tk):
    b, sq, _ = q.shape
    n_layers, _, past, heads, hd = k_cache.shape
    assert past % tk == 0 and sq <= tk and heads == SB_HEADS and hd == SB_HEAD_DIM
    key_minor = lambda c: jnp.transpose(c, (0, 1, 3, 4, 2))
    cache = pl.BlockSpec((1, 1, heads, hd, past), lambda bi: (layer, bi, 0, 0, 0))
    return pl.pallas_call(
        functools.partial(_sb_attn_kernel, tk=tk),
        grid=(b,),
        in_specs=[_stream_block(sq, SB_WIDTH), _stream_block(sq, SB_WIDTH), _stream_block(sq, SB_WIDTH),
                  cache, cache],
        out_specs=_stream_block(sq, SB_WIDTH),
        out_shape=jax.ShapeDtypeStruct((b, sq, SB_WIDTH), BF16),
        compiler_params=_params("parallel"),
        name="sb_attn",
    )(q, k_new, v_new, key_minor(k_cache), key_minor(v_cache))


def _post_kernel(x_ref, o_ref, wo_ref, g_ref, wup_ref, wdn_ref, gfin_ref, y_ref, *, final, ff_blk):
    x1 = x_ref[...] + jnp.dot(o_ref[...], wo_ref[...], preferred_element_type=F32)
    h = _rms(x1, g_ref[...]).astype(BF16)
    acc = x1
    for c in range(0, D_FF, ff_blk):
        up = jnp.dot(h, wup_ref[:, c:c + ff_blk], preferred_element_type=F32)
        act = jnp.square(jnp.maximum(up, 0.0)).astype(BF16)
        acc = acc + jnp.dot(act, wdn_ref[c:c + ff_blk, :], preferred_element_type=F32)
    y_ref[...] = _rms(acc, gfin_ref[...]) if final else acc


def _post(x, o, w_o, g_mlp, w_up, w_dn, g_fin, *, final):
    n = x.shape[0]
    tm = _row_block(n)
    row = pl.BlockSpec((tm, D_MODEL), lambda i: (i, 0))
    kern = functools.partial(_post_kernel, final=final, ff_blk=1024)
    return pl.pallas_call(
        kern,
        grid=(n // tm,),
        in_specs=[row, row, _resident(w_o.shape), _resident(g_mlp.shape), _resident(w_up.shape),
                  _resident(w_dn.shape), _resident(g_fin.shape)],
        out_specs=row,
        out_shape=jax.ShapeDtypeStruct((n, D_MODEL), F32),
        compiler_params=_params("parallel"),
        name="post_final" if final else "post",
    )(x, o, w_o, g_mlp, w_up, w_dn, g_fin)


def _cast_kernel(x_ref, o_ref):
    o_ref[...] = x_ref[...].astype(o_ref.dtype)


def _cast_bf16(w):
    cols = w.shape[-1]
    rows = w.size // cols
    tm = rows
    while tm * cols * 4 > CAST_BLOCK_BYTES and tm % 32 == 0:
        tm //= 2
    spec = pl.BlockSpec((tm, cols), lambda i: (i, 0))
    out = pl.pallas_call(
        _cast_kernel, grid=(rows // tm,), in_specs=[spec], out_specs=spec,
        out_shape=jax.ShapeDtypeStruct((rows, cols), BF16),
        compiler_params=_params("parallel"), name="cast_bf16",
    )(w.reshape(rows, cols))
    return out.reshape(w.shape)


def _prep_weights(norm_mix, norm_mlp, norm_final, mla_w_in, mla_g_q, mla_w_qb, mla_g_kv,
                  mla_w_kvb, mla_w_o, sb_w_in, sb_w_o, mlp_w_up, mlp_w_down):
    half = MLA_ROPE // 2
    lat = MLA_Q_LORA + MLA_KV_LORA
    w_in_ext = jnp.concatenate(
        [mla_w_in, -mla_w_in[:, :, lat + half:], mla_w_in[:, :, lat:lat + half]], axis=2)
    n_mla = mla_w_qb.shape[0]
    qb = mla_w_qb.reshape(n_mla, MLA_Q_LORA, MLA_HEADS, MLA_NOPE + MLA_ROPE)
    qb_rot = jnp.concatenate([-qb[..., MLA_NOPE + half:], qb[..., MLA_NOPE:MLA_NOPE + half]], axis=-1)
    w_qb_ext = jnp.concatenate(
        [qb[..., :MLA_NOPE].reshape(n_mla, MLA_Q_LORA, MLA_QN),
         qb[..., MLA_NOPE:].reshape(n_mla, MLA_Q_LORA, MLA_QR),
         qb_rot.reshape(n_mla, MLA_Q_LORA, MLA_QR)], axis=2)
    kvb = mla_w_kvb.reshape(n_mla, MLA_KV_LORA, MLA_HEADS, MLA_NOPE + MLA_V)
    w_k = kvb[..., :MLA_NOPE].reshape(n_mla, MLA_KV_LORA, MLA_QN).astype(BF16)
    w_v = kvb[..., MLA_NOPE:].reshape(n_mla, MLA_KV_LORA, MLA_VW).astype(BF16)
    sb_w_in = _cast_bf16(sb_w_in)
    return dict(
        norm_mix=norm_mix[:, None, :], norm_mlp=norm_mlp[:, None, :], norm_final=norm_final[None, :],
        w_in_ext=w_in_ext.astype(BF16), g_q=mla_g_q[:, None, :], g_kv=mla_g_kv[:, None, :],
        w_qb_ext=w_qb_ext.astype(BF16), w_k=w_k, w_v=w_v, w_vt=jnp.swapaxes(w_v, 1, 2),
        mla_w_o=_cast_bf16(mla_w_o), sb_w_in=sb_w_in,
        sb_w_vt=jnp.swapaxes(sb_w_in[:, :, 2 * SB_WIDTH:], 1, 2), sb_w_o=_cast_bf16(sb_w_o),
        w_up=_cast_bf16(mlp_w_up), w_dn=_cast_bf16(mlp_w_down))


def _rope_tables(pos, reps):
    inv = ROPE_THETA ** (-jnp.arange(0, MLA_ROPE, 2, dtype=F32) / MLA_ROPE)
    ang = pos.astype(F32)[:, None] * inv[None, :]
    tile = lambda t: jnp.tile(t, (reps, LANES // (MLA_ROPE // 2)))
    return tile(jnp.cos(ang)), tile(jnp.sin(ang))


def _trunk(x, caches, w):
    b, s, _ = x.shape
    n = b * s
    prompt = caches is None
    past = 0 if prompt else caches[0].shape[2]
    if prompt:
        mla_t = _row_block(s, MLA_TILE)
        sb_t = _row_block(s // 2, SB_TILE)
    pos = past + jnp.arange(s, dtype=jnp.int32)
    cos_t, sin_t = _rope_tables(pos, 1 if prompt else b)
    xf = x.reshape(n, D_MODEL)
    ckv_l, kr_l, kv_slabs = [], [], None
    for i in range(DEPTH):
        j = i // 2
        if i % 2 == 0:
            ckv, krope, kr2, qn, qr = _mla_proj(xf, w["norm_mix"][i], w["w_in_ext"][j], w["g_q"][j],
                                                w["g_kv"][j], w["w_qb_ext"][j], cos_t, sin_t)
            ckv_l.append(ckv.reshape(b, s, MLA_KV_LORA))
            krope = krope.reshape(b, s, MLA_ROPE)
            kr_l.append(krope)
            qn = qn.reshape(b, s, MLA_QN)
            qr = qr.reshape(b, s, MLA_QR)
            if prompt:
                kn, vt = _mla_kv(ckv, w["w_k"][j], w["w_vt"][j], tm=mla_t, blocks_per_batch=s // mla_t)
                o = _mla_attn_t(qn, qr, kn.reshape(b, s, MLA_QN), kr2.reshape(2, b, s, LANES), vt,
                                tq=mla_t, tk=mla_t)
            else:
                o = _mla_attn(qn, qr, ckv_l[-1], krope, caches[0], caches[1], j, w["w_k"][j], w["w_v"][j])
            w_o = w["mla_w_o"][j]
        else:
            proj = functools.partial(_sb_proj, xf, w["norm_mix"][i], w["sb_w_in"][j], w["sb_w_vt"][j],
                                     j, DEPTH // 2, kv_slabs)
            if prompt:
                q, k, v, kb, vb = proj(tm=sb_t, blocks_per_batch=s // sb_t)
                o = _sb_attn_t(q.reshape(b, s, SB_WIDTH), kb.reshape(b, s, SB_WIDTH), vb, tq=sb_t)
            else:
                q, k, v, kb, vb = proj(tm=_row_block(n))
                rows = lambda t: t.reshape(b, s, SB_WIDTH)
                o = _sb_attn(rows(q), rows(kb), rows(vb), caches[2], caches[3], j, tk=LANES)
            kv_slabs = (k, v)
            w_o = w["sb_w_o"][j]
        xf = _post(xf, o.reshape(n, D_MODEL), w_o, w["norm_mlp"][i], w["w_up"][i], w["w_dn"][i],
                   w["norm_final"], final=(i == DEPTH - 1))
    heads = lambda t: t.reshape(DEPTH // 2, b, s, SB_HEADS, SB_HEAD_DIM)
    return (xf.reshape(b, s, D_MODEL), jnp.stack(ckv_l), jnp.stack(kr_l), heads(kv_slabs[0]),
            heads(kv_slabs[1]))


def kernel(x_prompt, x_sample, cache_mla_ckv, cache_mla_krope, cache_sb_k, cache_sb_v, norm_mix, norm_mlp, norm_final, mla_w_in, mla_g_q, mla_w_qb, mla_g_kv, mla_w_kvb, mla_w_o, sb_w_in, sb_w_o, mlp_w_up, mlp_w_down):
    w = _prep_weights(norm_mix, norm_mlp, norm_final, mla_w_in, mla_g_q, mla_w_qb, mla_g_kv,
                      mla_w_kvb, mla_w_o, sb_w_in, sb_w_o, mlp_w_up, mlp_w_down)
    y_p, p_ckv, p_kr, p_k, p_v = _trunk(x_prompt, None, w)
    y_s, s_ckv, s_kr, s_k, s_v = _trunk(
        x_sample, (cache_mla_ckv, cache_mla_krope, cache_sb_k, cache_sb_v), w)
    return (y_p, y_s, p_ckv, p_kr, p_k, p_v, s_ckv, s_kr, s_k, s_v)
```
